```python
import math
import jax, jax.numpy as jnp
from jax import lax
import numpy as np

D_MODEL = 1024
BATCH = 16
SEQ = 4096
DEPTH = 2

GRID_W = 64
CTX_LEN = 256
Q_BLOCK = 128
EPS = 1e-6
ROPE_BASE = 10000.0

H_A = 8
Q_LORA = 384
KV_LORA = 256
NOPE_A = 64
ROPE_A = 32
V_A = 64

H_B = 4
DH_B = 64
V_B = 2 * DH_B

D_FF = 2816
CONV_W = 3

QKV_B = H_B * 2 * DH_B
IN_W = Q_LORA + KV_LORA + ROPE_A + 3 * QKV_B + 2 * D_MODEL

kernel_name = "hybrid_mla_diffattn_convffn_prefix_ctx"


def rms_norm(x, g):
    xf = x.astype(jnp.float32)
    y = xf * lax.rsqrt(jnp.mean(xf * xf, axis=-1, keepdims=True) + EPS)
    return (y * g.astype(jnp.float32)).astype(x.dtype)


def modulate(h, shift, scale):
    return h * (1.0 + scale) + shift


def axial_rope_tables(rows, cols, dr, dtype):
    q = dr // 4
    freqs = ROPE_BASE ** (-jnp.arange(q, dtype=jnp.float32) / q)
    ang = jnp.concatenate([rows[:, None] * freqs, cols[:, None] * freqs], axis=-1)
    return jnp.cos(ang)[:, None, :].astype(dtype), jnp.sin(ang)[:, None, :].astype(dtype)


def apply_axial_rope(x, cos, sin):
    q = x.shape[-1] // 4
    xr1, xr2, xc1, xc2 = jnp.split(x, 4, axis=-1)
    cr, cc = cos[..., :q], cos[..., q:]
    sr, sc = sin[..., :q], sin[..., q:]
    return jnp.concatenate([xr1 * cr - xr2 * sr, xr1 * sr + xr2 * cr,
                            xc1 * cc - xc2 * sc, xc1 * sc + xc2 * cc], axis=-1)


def map_query_blocks(fn, qs):
    B, S = qs[0].shape[:2]
    nb = S // Q_BLOCK
    blk = tuple(q.reshape(B, nb, Q_BLOCK, *q.shape[2:]).swapaxes(0, 1) for q in qs)
    out = lax.map(lambda qb: fn(*qb), blk)
    return out.swapaxes(0, 1).reshape(B, S, *out.shape[3:])


def mla_attend(qn, qr, kn, kr, v):
    scale = (NOPE_A + ROPE_A) ** -0.5

    def one(qnb, qrb):
        s = (jnp.einsum('bqhd,bkhd->bhqk', qnb, kn) + jnp.einsum('bqhr,bkr->bhqk', qrb, kr)).astype(jnp.float32)
        p = jax.nn.softmax(s * scale, axis=-1).astype(v.dtype)
        return jnp.einsum('bhqk,bkhe->bqhe', p, v)

    return map_query_blocks(one, (qn, qr))


def diff_attend(q1, q2, k1, k2, v, lam):
    scale = DH_B ** -0.5

    def one(q1b, q2b):
        s1 = jnp.einsum('bqhd,bkhd->bhqk', q1b, k1).astype(jnp.float32) * scale
        s2 = jnp.einsum('bqhd,bkhd->bhqk', q2b, k2).astype(jnp.float32) * scale
        p = jax.nn.softmax(s1, axis=-1) - lam * jax.nn.softmax(s2, axis=-1)
        return jnp.einsum('bhqk,bkhe->bqhe', p.astype(v.dtype), v)

    return map_query_blocks(one, (q1, q2))


def mixer_inputs(h, p, rope_a, rope_b):
    B, S, _ = h.shape
    z = h @ p['w_in']
    cuts = np.cumsum([Q_LORA, KV_LORA, ROPE_A, QKV_B, QKV_B, QKV_B, D_MODEL]).tolist()
    c_q, c_kv, k_r, dq, dk, dv, ga, gb = jnp.split(z, cuts, axis=-1)
    q = (rms_norm(c_q, p['g_q']) @ p['w_uq']).reshape(B, S, H_A, NOPE_A + ROPE_A)
    qn, qr = q[..., :NOPE_A], q[..., NOPE_A:]
    kv = (rms_norm(c_kv, p['g_kv']) @ p['w_ukv']).reshape(B, S, H_A, NOPE_A + V_A)
    kn, va = kv[..., :NOPE_A], kv[..., NOPE_A:]
    kr = k_r[:, :, None, :]
    dq = dq.reshape(B, S, H_B, 2, DH_B)
    dk = dk.reshape(B, S, H_B, 2, DH_B)
    q1, q2 = dq[..., 0, :], dq[..., 1, :]
    k1, k2 = dk[..., 0, :], dk[..., 1, :]
    vb = dv.reshape(B, S, H_B, V_B)
    if rope_a is not None:
        qr = apply_axial_rope(qr, *rope_a)
        kr = apply_axial_rope(kr, *rope_a)
        q1, q2, k1, k2 = (apply_axial_rope(t, *rope_b) for t in (q1, q2, k1, k2))
    return {'qn': qn, 'qr': qr, 'kn': kn, 'kr': kr[:, :, 0, :], 'va': va,
            'q1': q1, 'q2': q2, 'k1': k1, 'k2': k2, 'vb': vb, 'ga': ga, 'gb': gb}


def mixer_outputs(o_a, o_b, ga, gb, p, lam_init):
    B, S = o_a.shape[:2]
    y_a = o_a.reshape(B, S, H_A * V_A) @ p['w_br_a']
    o_b = rms_norm(o_b, p['g_sub']) * (1.0 - lam_init)
    y_b = o_b.reshape(B, S, H_B * V_B) @ p['w_br_b']
    merged = jax.nn.sigmoid(ga) * y_a + jax.nn.sigmoid(gb) * y_b
    return merged @ p['w_out']


def conv_ffn(h, p):
    u = h @ p['w_up']
    C = u.shape[-1]
    u = lax.conv_general_dilated(u, p['conv_w'][:, None, :], window_strides=(1,),
                                 padding=((CONV_W // 2, CONV_W // 2),),
                                 dimension_numbers=('NWC', 'WIO', 'NWC'),
                                 feature_group_count=C) + p['conv_b']
    a, b = jnp.split(u, 2, axis=-1)
    return (jax.nn.silu(a) * b) @ p['w_down']


def setup_inputs(seed: int = 0) -> dict:
    key = jax.random.key(seed)
    ks = jax.random.split(key, 32)
    L, D, F = DEPTH, D_MODEL, D_FF

    def nrm(k, shape, scale):
        return jax.random.normal(k, shape, jnp.float32) * scale

    def gain(k, shape):
        return 1.0 + 0.02 * jax.random.normal(k, shape, jnp.float32)

    return {
        'x': nrm(ks[0], (BATCH, SEQ, D), 1.0),
        'c': nrm(ks[1], (BATCH, D), 1.0),
        'ctx': nrm(ks[2], (BATCH, CTX_LEN, D), 1.0),
        'c_ctx': nrm(ks[3], (D,), 1.0),
        'w_ada': nrm(ks[4], (L, D, 6 * D), 0.5 * D ** -0.5),
        'b_ada': nrm(ks[5], (L, 6 * D), 0.01),
        'g_mix': gain(ks[6], (L, D)),
        'g_ffn': gain(ks[7], (L, D)),
        'w_in': nrm(ks[8], (L, D, IN_W), D ** -0.5),
        'g_q': gain(ks[9], (L, Q_LORA)),
        'w_uq': nrm(ks[10], (L, Q_LORA, H_A * (NOPE_A + ROPE_A)), Q_LORA ** -0.5),
        'g_kv': gain(ks[11], (L, KV_LORA)),
        'w_ukv': nrm(ks[12], (L, KV_LORA, H_A * (NOPE_A + V_A)), KV_LORA ** -0.5),
        'lam_q1': nrm(ks[13], (L, DH_B), 0.1),
        'lam_k1': nrm(ks[14], (L, DH_B), 0.1),
        'lam_q2': nrm(ks[15], (L, DH_B), 0.1),
        'lam_k2': nrm(ks[16], (L, DH_B), 0.1),
        'g_sub': gain(ks[17], (L, V_B)),
        'w_br_a': nrm(ks[18], (L, H_A * V_A, D), (H_A * V_A) ** -0.5),
        'w_br_b': nrm(ks[19], (L, H_B * V_B, D), (H_B * V_B) ** -0.5),
        'w_out': nrm(ks[20], (L, D, D), D ** -0.5),
        'w_up': nrm(ks[21], (L, D, 2 * F), D ** -0.5),
        'conv_w': nrm(ks[22], (L, CONV_W, 2 * F), CONV_W ** -0.5),
        'conv_b': nrm(ks[23], (L, 2 * F), 0.01),
        'w_down': nrm(ks[24], (L, F, D), F ** -0.5),
        'g_final': gain(ks[25], (D,)),
    }


def reference(x, c, ctx, c_ctx, w_ada, b_ada, g_mix, g_ffn, w_in, g_q, w_uq, g_kv, w_ukv,
              lam_q1, lam_k1, lam_q2, lam_k2, g_sub, w_br_a, w_br_b, w_out,
              w_up, conv_w, conv_b, w_down, g_final):
    B, S, D = x.shape
    ROWS = S // GRID_W
    rows = jnp.repeat(jnp.arange(ROWS), GRID_W).astype(jnp.float32)
    cols = jnp.tile(jnp.arange(GRID_W), ROWS).astype(jnp.float32)
    rope_a = axial_rope_tables(rows, cols, ROPE_A, x.dtype)
    rope_b = axial_rope_tables(rows, cols, DH_B, x.dtype)

    x_lat, x_ctx = x, ctx
    for l in range(DEPTH):
        last = l == DEPTH - 1
        p = {'w_in': w_in[l], 'g_q': g_q[l], 'w_uq': w_uq[l], 'g_kv': g_kv[l], 'w_ukv': w_ukv[l],
             'g_sub': g_sub[l], 'w_br_a': w_br_a[l], 'w_br_b': w_br_b[l], 'w_out': w_out[l],
             'w_up': w_up[l], 'conv_w': conv_w[l], 'conv_b': conv_b[l], 'w_down': w_down[l]}
        lam_init = 0.8 - 0.6 * math.exp(-0.3 * l)
        lam = (jnp.exp(jnp.sum(lam_q1[l].astype(jnp.float32) * lam_k1[l].astype(jnp.float32)))
               - jnp.exp(jnp.sum(lam_q2[l].astype(jnp.float32) * lam_k2[l].astype(jnp.float32)))
               + lam_init)

        mod_lat = (jax.nn.silu(c) @ w_ada[l] + b_ada[l])[:, None, :]
        mod_ctx = jax.nn.silu(c_ctx) @ w_ada[l] + b_ada[l]
        sh1, sc1, gt1, sh2, sc2, gt2 = jnp.split(mod_lat, 6, axis=-1)
        csh1, csc1, cgt1, csh2, csc2, cgt2 = jnp.split(mod_ctx, 6, axis=-1)

        h_lat = modulate(rms_norm(x_lat, g_mix[l]), sh1, sc1)
        h_ctx = modulate(rms_norm(x_ctx, g_mix[l]), csh1, csc1)
        pl = mixer_inputs(h_lat, p, rope_a, rope_b)
        pc = mixer_inputs(h_ctx, p, None, None)
        cat = lambda name: jnp.concatenate([pc[name], pl[name]], axis=1)
        o_a = mla_attend(pl['qn'], pl['qr'], cat('kn'), cat('kr'), cat('va'))
        o_b = diff_attend(pl['q1'], pl['q2'], cat('k1'), cat('k2'), cat('vb'), lam)
        o_lat = mixer_outputs(o_a, o_b, pl['ga'], pl['gb'], p, lam_init)
        if not last:
            oc_a = mla_attend(pc['qn'], pc['qr'], pc['kn'], pc['kr'], pc['va'])
            oc_b = diff_attend(pc['q1'], pc['q2'], pc['k1'], pc['k2'], pc['vb'], lam)
            o_ctx = mixer_outputs(oc_a, oc_b, pc['ga'], pc['gb'], p, lam_init)
            x_ctx = x_ctx + cgt1 * o_ctx
        x_lat = x_lat + gt1 * o_lat

        h_lat = modulate(rms_norm(x_lat, g_ffn[l]), sh2, sc2)
        x_lat = x_lat + gt2 * conv_ffn(h_lat, p)
        if not last:
            h_ctx = modulate(rms_norm(x_ctx, g_ffn[l]), csh2, csc2)
            x_ctx = x_ctx + cgt2 * conv_ffn(h_ctx, p)

    return rms_norm(x_lat, g_final)
```

```python
import functools
import math

import jax
import jax.numpy as jnp
from jax import lax
from jax.experimental import pallas as pl
from jax.experimental.pallas import tpu as pltpu

D_MODEL = 1024
GRID_W = 64
EPS = 1e-6
ROPE_BASE = 10000.0

H_A = 8
Q_LORA = 384
KV_LORA = 256
NOPE_A = 64
ROPE_A = 32
V_A = 64

H_B = 4
DH_B = 64
V_B = 2 * DH_B

D_FF = 2816
CONV_W = 3
QKV_B = H_B * 2 * DH_B

LANES = 128
HALO = 8
VMEM_LIMIT = 56 * 1024 * 1024

C_Q = 0
C_KV = C_Q + Q_LORA
C_KR = C_KV + KV_LORA
C_DQ = C_KR + LANES
C_DK = C_DQ + QKV_B
C_DV = C_DK + QKV_B
C_GA = C_DV + QKV_B
C_GB = C_GA + D_MODEL
IN_WP = C_GB + D_MODEL

BF16 = jnp.bfloat16
F32 = jnp.float32


def _params(n_grid):
    return pltpu.CompilerParams(dimension_semantics=("parallel",) * n_grid,
                                vmem_limit_bytes=VMEM_LIMIT)


def _rms(v, g):
    ms = jnp.mean(v * v, axis=-1, keepdims=True)
    return v * lax.rsqrt(ms + EPS) * g


def _swap_groups(v, q):
    n = v.shape[-1]
    lane = lax.broadcasted_iota(jnp.int32, v.shape, 1)
    ahead = pltpu.roll(v, n - q, 1)
    behind = pltpu.roll(v, q, 1)
    return jnp.where((lane & (2 * q - 1)) < q, ahead, behind)


def _rope(v, cos, sin, q):
    return v * cos + _swap_groups(v, q) * sin


def _ada_kernel(c_ref, w_ref, b_ref, o_ref):
    a = c_ref[...]
    a = a * jax.nn.sigmoid(a)
    w = w_ref[0]
    a_hi = a.astype(BF16)
    a_lo = (a - a_hi.astype(F32)).astype(BF16)
    w_hi = w.astype(BF16)
    w_lo = (w - w_hi.astype(F32)).astype(BF16)
    dot = functools.partial(jnp.dot, preferred_element_type=F32)
    o_ref[0] = dot(a_hi, w_hi) + dot(a_lo, w_hi) + dot(a_hi, w_lo) + b_ref[0]


def _ada(cc, w_ada, b_ada):
    L, D, N = w_ada.shape
    R = cc.shape[0]
    tn = 1024
    return pl.pallas_call(
        _ada_kernel,
        grid=(L, N // tn),
        in_specs=[pl.BlockSpec((R, D), lambda l, j: (0, 0)),
                  pl.BlockSpec((1, D, tn), lambda l, j: (l, 0, j)),
                  pl.BlockSpec((1, 1, tn), lambda l, j: (l, 0, j))],
        out_specs=pl.BlockSpec((1, R, tn), lambda l, j: (l, 0, j)),
        out_shape=jax.ShapeDtypeStruct((L, R, N), F32),
        compiler_params=_params(2),
        name="adaln",
    )(cc, w_ada, b_ada.reshape(L, 1, N))


def _proj_kernel(x_ref, mod_ref, gmix_ref, win_ref, gq_ref, wuq_ref, gkv_ref, wk_ref, wv_ref, tab_ref,
                 qa_ref, ka_ref, va_ref, qb_ref, kb1_ref, kb2_ref, vb_ref, sga_ref, sgb_ref):
    dot = functools.partial(jnp.dot, preferred_element_type=F32)
    x = x_ref[0]
    mod = mod_ref[0]
    h = (_rms(x, gmix_ref[...]) * (1.0 + mod[1:2]) + mod[0:1]).astype(BF16)
    tm = x.shape[0]
    lane = lax.broadcasted_iota(jnp.int32, (tm, LANES), 1)
    low = lane < (LANES // 2)
    tab = lambda k: tab_ref[:, k * LANES:(k + 1) * LANES]
    slot = lambda k: slice(k * LANES, (k + 1) * LANES)

    cq = dot(h, win_ref[:, C_Q:C_Q + Q_LORA])
    q = dot(_rms(cq, gq_ref[...]).astype(BF16), wuq_ref[...])
    for hd in range(H_A):
        qa_ref[0, :, slot(hd)] = _rope(q[:, slot(hd)], tab(0), tab(1), ROPE_A // 4).astype(BF16)
    ckv = _rms(dot(h, win_ref[:, C_KV:C_KV + KV_LORA]), gkv_ref[...]).astype(BF16)
    kn = dot(ckv, wk_ref[...])
    vv = dot(ckv, wv_ref[...])
    kr = _rope(dot(h, win_ref[:, C_KR:C_KR + LANES]), tab(2), tab(3), ROPE_A // 4)
    for hd in range(H_A):
        ka_ref[0, :, slot(hd)] = (kn[:, slot(hd)] + kr).astype(BF16)
        ones = jnp.where(low, 0.0, 1.0) if hd % 2 == 0 else jnp.where(low, 1.0, 0.0)
        va_ref[0, :, slot(hd)] = (vv[:, slot(hd)] + ones).astype(BF16)

    dq = dot(h, win_ref[:, C_DQ:C_DQ + QKV_B])
    dk = dot(h, win_ref[:, C_DK:C_DK + QKV_B])
    for hd in range(H_B):
        qb_ref[0, :, slot(hd)] = _rope(dq[:, slot(hd)], tab(4), tab(5), DH_B // 4).astype(BF16)
        k = _rope(dk[:, slot(hd)], tab(6), tab(7), DH_B // 4)
        kb1_ref[0, :, slot(hd)] = jnp.where(low, k, 0.0).astype(BF16)
        kb2_ref[0, :, slot(hd)] = jnp.where(low, 0.0, k).astype(BF16)
    vb_ref[0] = dot(h, win_ref[:, C_DV:C_DV + QKV_B]).astype(BF16)

    sga_ref[0] = jax.nn.sigmoid(dot(h, win_ref[:, C_GA:C_GA + D_MODEL])).astype(sga_ref.dtype)
    sgb_ref[0] = jax.nn.sigmoid(dot(h, win_ref[:, C_GB:C_GB + D_MODEL])).astype(sgb_ref.dtype)


def _proj(x, mod_l, mod_row, g_mix, w, tab, tm, gate_dtype):
    B, S, D = x.shape
    const = lambda shape: pl.BlockSpec(shape, lambda i, b: (0,) * len(shape))
    tok = lambda n: pl.BlockSpec((1, tm, n), lambda i, b: (b, i, 0))
    wa, wb = H_A * LANES, H_B * LANES
    outs = [(wa, BF16), (wa, BF16), (wa, BF16), (wb, BF16), (wb, BF16), (wb, BF16), (wb, BF16),
            (D, gate_dtype), (D, gate_dtype)]
    return pl.pallas_call(
        _proj_kernel,
        grid=(S // tm, B),
        in_specs=[tok(D),
                  pl.BlockSpec((1, 6, D), lambda i, b: (mod_row(b), 0, 0)),
                  const((1, D)), const((D, IN_WP)), const((1, Q_LORA)), const((Q_LORA, wa)),
                  const((1, KV_LORA)), const((KV_LORA, wa)), const((KV_LORA, wa)),
                  pl.BlockSpec((tm, 8 * LANES), lambda i, b: (i, 0))],
        out_specs=[tok(n) for n, _ in outs],
        out_shape=[jax.ShapeDtypeStruct((B, S, n), dt) for n, dt in outs],
        compiler_params=_params(2),
        name="proj",
    )(x, mod_l, g_mix, w["w_in"], w["g_q"], w["w_uq"], w["g_kv"], w["w_k"], w["w_v"], tab)


def _qk(q, k):
    return lax.dot_general(q, k, (((1,), (1,)), ((), ())), preferred_element_type=F32)


def _mla_kernel(*refs, n_chunks, tk):
    if n_chunks:
        q_ref, kc_ref, vc_ref, k_ref, v_ref, o_ref = refs
    else:
        q_ref, kc_ref, vc_ref, o_ref = refs
    accs = []
    for hh in range(2):
        sl = slice(hh * LANES, (hh + 1) * LANES)
        q = q_ref[0, :, sl]
        s = _qk(q, kc_ref[0, :, sl])
        m = jnp.max(s, axis=-1, keepdims=True)
        acc = jnp.dot(jnp.exp(s - m).astype(BF16), vc_ref[0, :, sl], preferred_element_type=F32)

        def body(i, carry, q=q, sl=sl):
            m, acc = carry
            off = pl.multiple_of(i * tk, tk)
            s = _qk(q, k_ref[0, pl.ds(off, tk), sl])
            m_new = jnp.maximum(m, jnp.max(s, axis=-1, keepdims=True))
            p = jnp.exp(s - m_new).astype(BF16)
            acc = jnp.exp(m - m_new) * acc + jnp.dot(p, v_ref[0, pl.ds(off, tk), sl],
                                                     preferred_element_type=F32)
            return m_new, acc

        if n_chunks:
            m, acc = lax.fori_loop(0, n_chunks, body, (m, acc))
        accs.append(acc)
    lane = lax.broadcasted_iota(jnp.int32, accs[0].shape, 1)
    o = jnp.where(lane < LANES // 2,
                  accs[0] / pltpu.roll(accs[0], LANES // 2, 1),
                  accs[1] / pltpu.roll(accs[1], LANES // 2, 1))
    o_ref[0] = o.astype(o_ref.dtype)


def _mla(q, kc, vc, k=None, v=None, *, tq, tk):
    B, Sq, _ = q.shape
    C = kc.shape[1]
    w = 2 * LANES
    qspec = pl.BlockSpec((1, tq, w), lambda b, h, i: (b, i, h))
    full = lambda n: pl.BlockSpec((1, n, w), lambda b, h, i: (b, 0, h))
    ins, specs, n_chunks = [q, kc, vc], [qspec, full(C), full(C)], 0
    if k is not None:
        S = k.shape[1]
        ins += [k, v]
        specs += [full(S), full(S)]
        n_chunks = S // tk
    return pl.pallas_call(
        functools.partial(_mla_kernel, n_chunks=n_chunks, tk=tk),
        grid=(B, H_A // 2, Sq // tq),
        in_specs=specs,
        out_specs=pl.BlockSpec((1, tq, LANES), lambda b, h, i: (b, i, h)),
        out_shape=jax.ShapeDtypeStruct((B, Sq, H_A * V_A), BF16),
        compiler_params=_params(3),
        name="mla",
    )(*ins)


def _diff_kernel(*refs, n_chunks, tk, lam_init):
    lq1_ref, lk1_ref, lq2_ref, lk2_ref, gsub_ref, q_ref, k1c_ref, k2c_ref, vc_ref = refs[:9]
    if n_chunks:
        k1_ref, k2_ref, v_ref, o_ref = refs[9:]
    else:
        (o_ref,) = refs[9:]
    lam = (jnp.exp(jnp.sum(lq1_ref[...] * lk1_ref[...], axis=-1, keepdims=True))
           - jnp.exp(jnp.sum(lq2_ref[...] * lk2_ref[...], axis=-1, keepdims=True)) + lam_init)
    q = q_ref[0]

    def first(k, v):
        s = _qk(q, k)
        m = jnp.max(s, axis=-1, keepdims=True)
        p = jnp.exp(s - m)
        return m, jnp.sum(p, axis=-1, keepdims=True), jnp.dot(p.astype(BF16), v, preferred_element_type=F32)

    def update(state, k, v):
        m, l, acc = state
        s = _qk(q, k)
        m_new = jnp.maximum(m, jnp.max(s, axis=-1, keepdims=True))
        p = jnp.exp(s - m_new)
        alpha = jnp.exp(m - m_new)
        return (m_new, alpha * l + jnp.sum(p, axis=-1, keepdims=True),
                alpha * acc + jnp.dot(p.astype(BF16), v, preferred_element_type=F32))

    vc = vc_ref[0]
    st1 = first(k1c_ref[0], vc)
    st2 = first(k2c_ref[0], vc)
    if n_chunks:
        def body(i, carry):
            st1, st2 = carry
            rows = pl.ds(pl.multiple_of(i * tk, tk), tk)
            v = v_ref[0, rows, :]
            return update(st1, k1_ref[0, rows, :], v), update(st2, k2_ref[0, rows, :], v)

        st1, st2 = lax.fori_loop(0, n_chunks, body, (st1, st2))
    o = st1[2] / st1[1] - lam * (st2[2] / st2[1])
    o_ref[0] = (_rms(o, gsub_ref[...]) * (1.0 - lam_init)).astype(o_ref.dtype)


def _diff(lams, g_sub, q, k1c, k2c, vc, k1=None, k2=None, v=None, *, tq, tk, lam_init):
    B, Sq, _ = q.shape
    C = k1c.shape[1]
    small = lambda n: pl.BlockSpec((1, n), lambda b, h, i: (0, 0))
    full = lambda n: pl.BlockSpec((1, n, LANES), lambda b, h, i: (b, 0, h))
    tile = pl.BlockSpec((1, tq, LANES), lambda b, h, i: (b, i, h))
    ins = list(lams) + [g_sub, q, k1c, k2c, vc]
    specs = [small(DH_B)] * 4 + [small(V_B), tile, full(C), full(C), full(C)]
    n_chunks = 0
    if k1 is not None:
        S = k1.shape[1]
        ins += [k1, k2, v]
        specs += [full(S)] * 3
        n_chunks = S // tk
    return pl.pallas_call(
        functools.partial(_diff_kernel, n_chunks=n_chunks, tk=tk, lam_init=lam_init),
        grid=(B, H_B, Sq // tq),
        in_specs=specs,
        out_specs=tile,
        out_shape=jax.ShapeDtypeStruct((B, Sq, H_B * V_B), BF16),
        compiler_params=_params(3),
        name="diff",
    )(*ins)


def _merge_kernel(x_ref, mod_ref, oa_ref, ob_ref, sga_ref, sgb_ref, wa_ref, wb_ref, wo_ref, o_ref):
    dot = functools.partial(jnp.dot, preferred_element_type=F32)
    ya = dot(oa_ref[0], wa_ref[...])
    yb = dot(ob_ref[0], wb_ref[...])
    merged = sga_ref[0].astype(F32) * ya + sgb_ref[0].astype(F32) * yb
    o_ref[0] = x_ref[0] + mod_ref[0][2:3] * dot(merged.astype(BF16), wo_ref[...])


def _merge(x, mod_l, mod_row, oa, ob, sga, sgb, w, tm):
    B, S, D = x.shape
    const = lambda shape: pl.BlockSpec(shape, lambda b, i: (0,) * len(shape))
    tok = lambda n: pl.BlockSpec((1, tm, n), lambda b, i: (b, i, 0))
    na, nb = H_A * V_A, H_B * V_B
    return pl.pallas_call(
        _merge_kernel,
        grid=(B, S // tm),
        in_specs=[tok(D), pl.BlockSpec((1, 6, D), lambda b, i: (mod_row(b), 0, 0)),
                  tok(na), tok(nb), tok(D), tok(D),
                  const((na, D)), const((nb, D)), const((D, D))],
        out_specs=tok(D),
        out_shape=jax.ShapeDtypeStruct((B, S, D), F32),
        compiler_params=_params(2),
        name="merge",
    )(x, mod_l, oa, ob, sga, sgb, w["w_br_a"], w["w_br_b"], w["w_out"])


def _ffn_kernel(*refs, tm, n_chunks, final):
    xp_ref, x_ref, xn_ref, mod_ref, g_ref, wa_ref, wb_ref, cwa_ref, cwb_ref, wd_ref = refs[:10]
    gfin_ref = refs[10] if final else None
    o_ref = refs[-1]
    dot = functools.partial(jnp.dot, preferred_element_type=F32)
    i = pl.program_id(1)
    mod = mod_ref[0]
    norm = lambda v: _rms(v, g_ref[...]) * (1.0 + mod[4:5]) + mod[3:4]
    x = x_ref[0]
    keep_prev = (i > 0).astype(F32)
    keep_next = (i < pl.num_programs(1) - 1).astype(F32)
    h = jnp.concatenate([norm(xp_ref[0]) * keep_prev, norm(x), norm(xn_ref[0]) * keep_next],
                        axis=0).astype(BF16)
    rows = tm + 2 * HALO

    def conv(u, cw):
        mid = slice(HALO, HALO + tm)
        return (pltpu.roll(u, 1, 0)[mid] * cw[0:1] + u[mid] * cw[1:2]
                + pltpu.roll(u, rows - 1, 0)[mid] * cw[2:3] + cw[3:4])

    def body(c, acc):
        a = conv(dot(h, wa_ref[c]), cwa_ref[c])
        b = conv(dot(h, wb_ref[c]), cwb_ref[c])
        act = (a * jax.nn.sigmoid(a) * b).astype(BF16)
        return acc + dot(act, wd_ref[c])

    y = lax.fori_loop(0, n_chunks, body, jnp.zeros((tm, x.shape[1]), F32))
    out = x + mod[5:6] * y
    if final:
        out = _rms(out, gfin_ref[...])
    o_ref[0] = out


def _ffn(x, mod_l, mod_row, g_ffn, w, tm, g_final=None):
    B, S, D = x.shape
    nc, _, fc = w["w_up_a"].shape
    const = lambda shape: pl.BlockSpec(shape, lambda b, i: (0,) * len(shape))
    r = tm // HALO
    last = S // HALO - 1
    ins = [x, x, x, mod_l, g_ffn, w["w_up_a"], w["w_up_b"], w["conv_a"], w["conv_b"], w["w_down"]]
    specs = [pl.BlockSpec((1, HALO, D), lambda b, i: (b, jnp.maximum(i * r - 1, 0), 0)),
             pl.BlockSpec((1, tm, D), lambda b, i: (b, i, 0)),
             pl.BlockSpec((1, HALO, D), lambda b, i: (b, jnp.minimum((i + 1) * r, last), 0)),
             pl.BlockSpec((1, 6, D), lambda b, i: (mod_row(b), 0, 0)),
             const((1, D)), const((nc, D, fc)), const((nc, D, fc)),
             const((nc, CONV_W + 1, fc)), const((nc, CONV_W + 1, fc)), const((nc, fc, D))]
    if g_final is not None:
        ins.append(g_final)
        specs.append(const((1, D)))
    return pl.pallas_call(
        functools.partial(_ffn_kernel, tm=tm, n_chunks=nc, final=g_final is not None),
        grid=(B, S // tm),
        in_specs=specs,
        out_specs=pl.BlockSpec((1, tm, D), lambda b, i: (b, i, 0)),
        out_shape=jax.ShapeDtypeStruct((B, S, D), F32),
        compiler_params=_params(2),
        name="ffn",
    )(*ins)


def _slot_pad(w, heads, width, left=0):
    k = w.shape[0]
    w = w.reshape(k, heads, width)
    w = jnp.pad(w, ((0, 0), (0, 0), (left, LANES - width - left)))
    return w.reshape(k, heads * LANES)


def _layer_weights(l, w_in, g_q, w_uq, g_kv, w_ukv, w_br_a, w_br_b, w_out, w_up, conv_w, conv_b, w_down, fc):
    D, F = D_MODEL, D_FF
    cuts = [0, Q_LORA, Q_LORA + KV_LORA, Q_LORA + KV_LORA + ROPE_A]
    wi = w_in[l]
    kr = jnp.pad(wi[:, cuts[2]:cuts[3]], ((0, 0), (NOPE_A, LANES - NOPE_A - ROPE_A)))
    w_in_p = jnp.concatenate([wi[:, :cuts[2]], kr, wi[:, cuts[3]:]], axis=1)
    ukv = w_ukv[l].reshape(KV_LORA, H_A, NOPE_A + V_A)
    kn, va = ukv[..., :NOPE_A], ukv[..., NOPE_A:]
    odd = (jnp.arange(H_A) % 2 == 1)[None, :, None]
    v_slots = jnp.where(odd, jnp.pad(va, ((0, 0), (0, 0), (LANES - V_A, 0))),
                        jnp.pad(va, ((0, 0), (0, 0), (0, LANES - V_A))))
    nc = F // fc
    chunks = lambda w: w.reshape(w.shape[0], nc, fc).swapaxes(0, 1)
    cw = jnp.concatenate([conv_w[l], conv_b[l][None]], axis=0)
    return {
        "w_in": w_in_p.astype(BF16),
        "g_q": g_q[l][None], "g_kv": g_kv[l][None],
        "w_uq": _slot_pad(w_uq[l], H_A, NOPE_A + ROPE_A).astype(BF16),
        "w_k": _slot_pad(kn.reshape(KV_LORA, H_A * NOPE_A), H_A, NOPE_A).astype(BF16),
        "w_v": v_slots.reshape(KV_LORA, H_A * LANES).astype(BF16),
        "w_br_a": w_br_a[l].astype(BF16), "w_br_b": w_br_b[l].astype(BF16), "w_out": w_out[l].astype(BF16),
        "w_up_a": chunks(w_up[l][:, :F]).astype(BF16), "w_up_b": chunks(w_up[l][:, F:]).astype(BF16),
        "conv_a": chunks(cw[:, :F]), "conv_b": chunks(cw[:, F:]),
        "w_down": w_down[l].reshape(nc, fc, D).astype(BF16),
    }


def _rope_tables(S, rotary):
    def slot_tables(dr, left, reps):
        q = dr // 4
        if rotary:
            t = jnp.arange(S)
            rows = (t // GRID_W).astype(F32)
            cols = (t % GRID_W).astype(F32)
            freqs = ROPE_BASE ** (-jnp.arange(q, dtype=F32) / q)
            ar, ac = rows[:, None] * freqs, cols[:, None] * freqs
            cos = jnp.concatenate([jnp.cos(ar), jnp.cos(ar), jnp.cos(ac), jnp.cos(ac)], axis=-1)
            sin = jnp.concatenate([-jnp.sin(ar), jnp.sin(ar), -jnp.sin(ac), jnp.sin(ac)], axis=-1)
        else:
            cos, sin = jnp.ones((S, dr), F32), jnp.zeros((S, dr), F32)
        cos, sin = jnp.tile(cos, (1, reps)), jnp.tile(sin, (1, reps))
        pad = ((0, 0), (left, LANES - left - dr * reps))
        return jnp.pad(cos, pad, constant_values=1.0), jnp.pad(sin, pad)

    ca, sa = slot_tables(ROPE_A, NOPE_A, 1)
    cb, sb = slot_tables(DH_B, 0, 2)
    scale_a = (NOPE_A + ROPE_A) ** -0.5
    scale_b = DH_B ** -0.5
    return jnp.concatenate([ca * scale_a, sa * scale_a, ca, sa, cb * scale_b, sb * scale_b, cb, sb], axis=1)


def kernel(x, c, ctx, c_ctx, w_ada, b_ada, g_mix, g_ffn, w_in, g_q, w_uq, g_kv, w_ukv, lam_q1, lam_k1, lam_q2,
           lam_k2, g_sub, w_br_a, w_br_b, w_out, w_up, conv_w, conv_b, w_down, g_final):
    B, S, D = x.shape
    C = ctx.shape[1]
    depth = w_in.shape[0]
    tm_proj, tm_merge, tm_ffn, tq, tk, fc = 256, 512, 512, 256, 512, 256
    tm_proj, tm_merge, tm_ffn, tq, tk = (min(t, S) for t in (tm_proj, tm_merge, tm_ffn, tq, tk))
    gate_dtype = F32

    ctx_row = B
    n_rows = -(-(B + 1) // 8) * 8
    cc = jnp.zeros((n_rows, D), F32).at[:B].set(c).at[ctx_row].set(c_ctx)
    mod = _ada(cc, w_ada, b_ada).reshape(depth, n_rows, 6, D)
    lat_row = lambda b: b
    ctx_mod_row = lambda b: ctx_row

    tab_lat = _rope_tables(S, True)
    tab_ctx = _rope_tables(C, False)

    x_lat, x_ctx = x, ctx
    for l in range(depth):
        last = l == depth - 1
        lam_init = 0.8 - 0.6 * math.exp(-0.3 * l)
        w = _layer_weights(l, w_in, g_q, w_uq, g_kv, w_ukv, w_br_a, w_br_b, w_out, w_up, conv_w, conv_b,
                           w_down, fc)
        lams = [v[l][None] for v in (lam_q1, lam_k1, lam_q2, lam_k2)]
        gs = g_sub[l][None]
        gm, gf = g_mix[l][None], g_ffn[l][None]

        pc = _proj(x_ctx, mod[l], ctx_mod_row, gm, w, tab_ctx, min(tm_proj, C), gate_dtype)
        pq = _proj(x_lat, mod[l], lat_row, gm, w, tab_lat, tm_proj, gate_dtype)
        qa, ka, va, qb, kb1, kb2, vb, sga, sgb = pq
        cqa, cka, cva, cqb, ckb1, ckb2, cvb, csga, csgb = pc

        o_a = _mla(qa, cka, cva, ka, va, tq=tq, tk=tk)
        o_b = _diff(lams, gs, qb, ckb1, ckb2, cvb, kb1, kb2, vb, tq=tq, tk=tk, lam_init=lam_init)
        if not last:
            oc_a = _mla(cqa, cka, cva, tq=C, tk=tk)
            oc_b = _diff(lams, gs, cqb, ckb1, ckb2, cvb, tq=C, tk=tk, lam_init=lam_init)
            x_ctx = _merge(x_ctx, mod[l], ctx_mod_row, oc_a, oc_b, csga, csgb, w, C)
            x_ctx = _ffn(x_ctx, mod[l], ctx_mod_row, gf, w, C)
        x_lat = _merge(x_lat, mod[l], lat_row, o_a, o_b, sga, sgb, w, tm_merge)
        x_lat = _ffn(x_lat, mod[l], lat_row, gf, w, tm_ffn, g_final[None] if last else None)
    return x_lat
```

```python
import functools
import math

import jax
import jax.numpy as jnp
from jax import lax
from jax.experimental import pallas as pl
from jax.experimental.pallas import tpu as pltpu

D_MODEL = 1024
GRID_W = 64
EPS = 1e-6
ROPE_BASE = 10000.0

H_A = 8
Q_LORA = 384
KV_LORA = 256
NOPE_A = 64
ROPE_A = 32
V_A = 64

H_B = 4
DH_B = 64
V_B = 2 * DH_B

D_FF = 2816
CONV_W = 3
QKV_B = H_B * 2 * DH_B

LANES = 128
HALO = 8
VMEM_LIMIT = 56 * 1024 * 1024

C_Q = 0
C_KV = C_Q + Q_LORA
C_KR = C_KV + KV_LORA
C_DQ = C_KR + LANES
C_DK = C_DQ + QKV_B
C_DV = C_DK + QKV_B
C_GA = C_DV + QKV_B
C_GB = C_GA + D_MODEL
IN_WP = C_GB + D_MODEL

BF16 = jnp.bfloat16
F32 = jnp.float32


def _params(n_grid):
    return pltpu.CompilerParams(dimension_semantics=("parallel",) * n_grid,
                                vmem_limit_bytes=VMEM_LIMIT)


def _rms(v, g):
    ms = jnp.mean(v * v, axis=-1, keepdims=True)
    return v * lax.rsqrt(ms + EPS) * g


def _swap_groups(v, q):
    n = v.shape[-1]
    lane = lax.broadcasted_iota(jnp.int32, v.shape, 1)
    ahead = pltpu.roll(v, n - q, 1)
    behind = pltpu.roll(v, q, 1)
    return jnp.where((lane & (2 * q - 1)) < q, ahead, behind)


def _rope(v, cos, sin, q):
    return v * cos + _swap_groups(v, q) * sin


def _ada_kernel(c_ref, w_ref, b_ref, o_ref):
    a = c_ref[...]
    a = a * jax.nn.sigmoid(a)
    w = w_ref[0]
    a_hi = a.astype(BF16)
    a_lo = (a - a_hi.astype(F32)).astype(BF16)
    w_hi = w.astype(BF16)
    w_lo = (w - w_hi.astype(F32)).astype(BF16)
    dot = functools.partial(jnp.dot, preferred_element_type=F32)
    o_ref[0] = dot(a_hi, w_hi) + dot(a_lo, w_hi) + dot(a_hi, w_lo) + b_ref[0]


def _ada(cc, w_ada, b_ada):
    L, D, N = w_ada.shape
    R = cc.shape[0]
    tn = 1024
    return pl.pallas_call(
        _ada_kernel,
        grid=(L, N // tn),
        in_specs=[pl.BlockSpec((R, D), lambda l, j: (0, 0)),
                  pl.BlockSpec((1, D, tn), lambda l, j: (l, 0, j)),
                  pl.BlockSpec((1, 1, tn), lambda l, j: (l, 0, j))],
        out_specs=pl.BlockSpec((1, R, tn), lambda l, j: (l, 0, j)),
        out_shape=jax.ShapeDtypeStruct((L, R, N), F32),
        compiler_params=_params(2),
        name="adaln",
    )(cc, w_ada, b_ada.reshape(L, 1, N))


def _proj_kernel(x_ref, mod_ref, gmix_ref, win_ref, gq_ref, wuq_ref, gkv_ref, wk_ref, wv_ref, tab_ref,
                 qa_ref, ka_ref, va_ref, qb_ref, kb1_ref, kb2_ref, vb_ref, sga_ref, sgb_ref):
    dot = functools.partial(jnp.dot, preferred_element_type=F32)
    x = x_ref[0]
    mod = mod_ref[0]
    h = (_rms(x, gmix_ref[...]) * (1.0 + mod[1:2]) + mod[0:1]).astype(BF16)
    tm = x.shape[0]
    lane = lax.broadcasted_iota(jnp.int32, (tm, LANES), 1)
    low = lane < (LANES // 2)
    tab = lambda k: tab_ref[:, k * LANES:(k + 1) * LANES]
    slot = lambda k: slice(k * LANES, (k + 1) * LANES)

    cq = dot(h, win_ref[:, C_Q:C_Q + Q_LORA])
    q = dot(_rms(cq, gq_ref[...]).astype(BF16), wuq_ref[...])
    for hd in range(H_A):
        qa_ref[0, :, slot(hd)] = _rope(q[:, slot(hd)], tab(0), tab(1), ROPE_A // 4).astype(BF16)
    ckv = _rms(dot(h, win_ref[:, C_KV:C_KV + KV_LORA]), gkv_ref[...]).astype(BF16)
    kn = dot(ckv, wk_ref[...])
    vv = dot(ckv, wv_ref[...])
    kr = _rope(dot(h, win_ref[:, C_KR:C_KR + LANES]), tab(2), tab(3), ROPE_A // 4)
    for hd in range(H_A):
        ka_ref[0, :, slot(hd)] = (kn[:, slot(hd)] + kr).astype(BF16)
        ones = jnp.where(low, 0.0, 1.0) if hd % 2 == 0 else jnp.where(low, 1.0, 0.0)
        va_ref[0, :, slot(hd)] = (vv[:, slot(hd)] + ones).astype(BF16)

    dq = dot(h, win_ref[:, C_DQ:C_DQ + QKV_B])
    dk = dot(h, win_ref[:, C_DK:C_DK + QKV_B])
    for hd in range(H_B):
        qb_ref[0, :, slot(hd)] = _rope(dq[:, slot(hd)], tab(4), tab(5), DH_B // 4).astype(BF16)
        k = _rope(dk[:, slot(hd)], tab(6), tab(7), DH_B // 4)
        kb1_ref[0, :, slot(hd)] = jnp.where(low, k, 0.0).astype(BF16)
        kb2_ref[0, :, slot(hd)] = jnp.where(low, 0.0, k).astype(BF16)
    vb_ref[0] = dot(h, win_ref[:, C_DV:C_DV + QKV_B]).astype(BF16)

    sga_ref[0] = jax.nn.sigmoid(dot(h, win_ref[:, C_GA:C_GA + D_MODEL])).astype(sga_ref.dtype)
    sgb_ref[0] = jax.nn.sigmoid(dot(h, win_ref[:, C_GB:C_GB + D_MODEL])).astype(sgb_ref.dtype)


def _proj(x, mod_l, mod_row, g_mix, w, tab, tm, gate_dtype):
    B, S, D = x.shape
    const = lambda shape: pl.BlockSpec(shape, lambda i, b: (0,) * len(shape))
    tok = lambda n: pl.BlockSpec((1, tm, n), lambda i, b: (b, i, 0))
    wa, wb = H_A * LANES, H_B * LANES
    outs = [(wa, BF16), (wa, BF16), (wa, BF16), (wb, BF16), (wb, BF16), (wb, BF16), (wb, BF16),
            (D, gate_dtype), (D, gate_dtype)]
    return pl.pallas_call(
        _proj_kernel,
        grid=(S // tm, B),
        in_specs=[tok(D),
                  pl.BlockSpec((1, 6, D), lambda i, b: (mod_row(b), 0, 0)),
                  const((1, D)), const((D, IN_WP)), const((1, Q_LORA)), const((Q_LORA, wa)),
                  const((1, KV_LORA)), const((KV_LORA, wa)), const((KV_LORA, wa)),
                  pl.BlockSpec((tm, 8 * LANES), lambda i, b: (i, 0))],
        out_specs=[tok(n) for n, _ in outs],
        out_shape=[jax.ShapeDtypeStruct((B, S, n), dt) for n, dt in outs],
        compiler_params=_params(2),
        name="proj",
    )(x, mod_l, g_mix, w["w_in"], w["g_q"], w["w_uq"], w["g_kv"], w["w_k"], w["w_v"], tab)


def _qk(q, k):
    return lax.dot_general(q, k, (((1,), (1,)), ((), ())), preferred_element_type=F32)


def _absorb(s, v, state, with_sum):
    smax = jnp.max(s, axis=-1, keepdims=True)
    m_new = smax if state is None else jnp.maximum(state[0], smax)
    p = jnp.exp2(s - m_new)
    pv = jnp.dot(p.astype(BF16), v, preferred_element_type=F32)
    if state is None:
        return (m_new, jnp.sum(p, axis=-1, keepdims=True), pv) if with_sum else (m_new, pv)
    alpha = jnp.exp2(state[0] - m_new)
    if with_sum:
        return m_new, alpha * state[1] + jnp.sum(p, axis=-1, keepdims=True), alpha * state[2] + pv
    return m_new, alpha * state[1] + pv


def _flash(n_streams, n_chunks, tk, score_ctx, score, value_ctx, value, with_sum, s_scr):
    streams = range(n_streams)
    states = tuple(_absorb(score_ctx(t), value_ctx(t), None, with_sum) for t in streams)
    if not n_chunks:
        return states
    assert n_chunks % 2 == 0
    rows = lambda i: pl.ds(pl.multiple_of(i * tk, tk), tk)

    def step(i, slot, states, prefetch):
        out = []
        for t in streams:
            if prefetch:
                s_scr[1 - slot, t] = score(t, rows(i + 1))
            out.append(_absorb(s_scr[slot, t], value(t, rows(i)), states[t], with_sum))
        return tuple(out)

    for t in streams:
        s_scr[0, t] = score(t, rows(0))
    states = lax.fori_loop(0, n_chunks // 2 - 1,
                           lambda j, st: step(2 * j + 1, 1, step(2 * j, 0, st, True), True), states)
    return step(n_chunks - 1, 1, step(n_chunks - 2, 0, states, True), False)


def _score_scratch(n_chunks, tq, tk):
    return pltpu.VMEM((2, 2, tq, tk) if n_chunks else (1, 1, 8, LANES), F32)


def _mla_kernel(*refs, n_chunks, tk):
    if n_chunks:
        q_ref, kc_ref, vc_ref, k_ref, v_ref, o_ref, s_scr = refs
    else:
        (q_ref, kc_ref, vc_ref, o_ref, s_scr), k_ref, v_ref = refs, None, None
    sl = lambda t: slice(t * LANES, (t + 1) * LANES)
    states = _flash(
        2, n_chunks, tk,
        score_ctx=lambda t: _qk(q_ref[0, :, sl(t)], kc_ref[0, :, sl(t)]),
        score=lambda t, rows: _qk(q_ref[0, :, sl(t)], k_ref[0, rows, sl(t)]),
        value_ctx=lambda t: vc_ref[0, :, sl(t)],
        value=lambda t, rows: v_ref[0, rows, sl(t)],
        with_sum=False, s_scr=s_scr)
    acc0, acc1 = states[0][1], states[1][1]
    lane = lax.broadcasted_iota(jnp.int32, acc0.shape, 1)
    o = jnp.where(lane < LANES // 2,
                  acc0 / pltpu.roll(acc0, LANES // 2, 1),
                  acc1 / pltpu.roll(acc1, LANES // 2, 1))
    o_ref[0] = o.astype(o_ref.dtype)


def _mla(q, kc, vc, k=None, v=None, *, tq, tk):
    B, Sq, _ = q.shape
    C = kc.shape[1]
    w = 2 * LANES
    qspec = pl.BlockSpec((1, tq, w), lambda b, h, i: (b, i, h))
    full = lambda n: pl.BlockSpec((1, n, w), lambda b, h, i: (b, 0, h))
    ins, specs, n_chunks = [q, kc, vc], [qspec, full(C), full(C)], 0
    if k is not None:
        S = k.shape[1]
        ins += [k, v]
        specs += [full(S), full(S)]
        n_chunks = S // tk
    return pl.pallas_call(
        functools.partial(_mla_kernel, n_chunks=n_chunks, tk=tk),
        grid=(B, H_A // 2, Sq // tq),
        in_specs=specs,
        out_specs=pl.BlockSpec((1, tq, LANES), lambda b, h, i: (b, i, h)),
        out_shape=jax.ShapeDtypeStruct((B, Sq, H_A * V_A), BF16),
        scratch_shapes=[_score_scratch(n_chunks, tq, tk)],
        compiler_params=_params(3),
        name="mla",
    )(*ins)


def _diff_kernel(*refs, n_chunks, tk, lam_init):
    lq1_ref, lk1_ref, lq2_ref, lk2_ref, gsub_ref, q_ref, k1c_ref, k2c_ref, vc_ref = refs[:9]
    if n_chunks:
        k1_ref, k2_ref, v_ref, o_ref, s_scr = refs[9:]
    else:
        (o_ref, s_scr), k1_ref, k2_ref, v_ref = refs[9:], None, None, None
    lam = (jnp.exp(jnp.sum(lq1_ref[...] * lk1_ref[...], axis=-1, keepdims=True))
           - jnp.exp(jnp.sum(lq2_ref[...] * lk2_ref[...], axis=-1, keepdims=True)) + lam_init)
    kc_refs, k_refs = (k1c_ref, k2c_ref), (k1_ref, k2_ref)
    (_, l1, o1), (_, l2, o2) = _flash(
        2, n_chunks, tk,
        score_ctx=lambda t: _qk(q_ref[0], kc_refs[t][0]),
        score=lambda t, rows: _qk(q_ref[0], k_refs[t][0, rows, :]),
        value_ctx=lambda t: vc_ref[0],
        value=lambda t, rows: v_ref[0, rows, :],
        with_sum=True, s_scr=s_scr)
    o = o1 / l1 - lam * (o2 / l2)
    o_ref[0] = (_rms(o, gsub_ref[...]) * (1.0 - lam_init)).astype(o_ref.dtype)


def _diff(lams, g_sub, q, k1c, k2c, vc, k1=None, k2=None, v=None, *, tq, tk, lam_init):
    B, Sq, _ = q.shape
    C = k1c.shape[1]
    small = lambda n: pl.BlockSpec((1, n), lambda b, h, i: (0, 0))
    full = lambda n: pl.BlockSpec((1, n, LANES), lambda b, h, i: (b, 0, h))
    tile = pl.BlockSpec((1, tq, LANES), lambda b, h, i: (b, i, h))
    ins = list(lams) + [g_sub, q, k1c, k2c, vc]
    specs = [small(DH_B)] * 4 + [small(V_B), tile, full(C), full(C), full(C)]
    n_chunks = 0
    if k1 is not None:
        S = k1.shape[1]
        ins += [k1, k2, v]
        specs += [full(S)] * 3
        n_chunks = S // tk
    return pl.pallas_call(
        functools.partial(_diff_kernel, n_chunks=n_chunks, tk=tk, lam_init=lam_init),
        grid=(B, H_B, Sq // tq),
        in_specs=specs,
        out_specs=tile,
        out_shape=jax.ShapeDtypeStruct((B, Sq, H_B * V_B), BF16),
        scratch_shapes=[_score_scratch(n_chunks, tq, tk)],
        compiler_params=_params(3),
        name="diff",
    )(*ins)


def _merge_kernel(x_ref, mod_ref, oa_ref, ob_ref, sga_ref, sgb_ref, wa_ref, wb_ref, wo_ref, o_ref):
    dot = functools.partial(jnp.dot, preferred_element_type=F32)
    ya = dot(oa_ref[0], wa_ref[...])
    yb = dot(ob_ref[0], wb_ref[...])
    merged = sga_ref[0].astype(F32) * ya + sgb_ref[0].astype(F32) * yb
    o_ref[0] = x_ref[0] + mod_ref[0][2:3] * dot(merged.astype(BF16), wo_ref[...])


def _merge(x, mod_l, mod_row, oa, ob, sga, sgb, w, tm):
    B, S, D = x.shape
    const = lambda shape: pl.BlockSpec(shape, lambda b, i: (0,) * len(shape))
    tok = lambda n: pl.BlockSpec((1, tm, n), lambda b, i: (b, i, 0))
    na, nb = H_A * V_A, H_B * V_B
    return pl.pallas_call(
        _merge_kernel,
        grid=(B, S // tm),
        in_specs=[tok(D), pl.BlockSpec((1, 6, D), lambda b, i: (mod_row(b), 0, 0)),
                  tok(na), tok(nb), tok(D), tok(D),
                  const((na, D)), const((nb, D)), const((D, D))],
        out_specs=tok(D),
        out_shape=jax.ShapeDtypeStruct((B, S, D), F32),
        compiler_params=_params(2),
        name="merge",
    )(x, mod_l, oa, ob, sga, sgb, w["w_br_a"], w["w_br_b"], w["w_out"])


def _ffn_kernel(*refs, tm, n_chunks, final):
    xp_ref, x_ref, xn_ref, mod_ref, g_ref, wa_ref, wb_ref, cwa_ref, cwb_ref, wd_ref = refs[:10]
    gfin_ref = refs[10] if final else None
    o_ref = refs[-1]
    dot = functools.partial(jnp.dot, preferred_element_type=F32)
    i = pl.program_id(1)
    mod = mod_ref[0]
    norm = lambda v: _rms(v, g_ref[...]) * (1.0 + mod[4:5]) + mod[3:4]
    x = x_ref[0]
    keep_prev = (i > 0).astype(F32)
    keep_next = (i < pl.num_programs(1) - 1).astype(F32)
    h = jnp.concatenate([norm(xp_ref[0]) * keep_prev, norm(x), norm(xn_ref[0]) * keep_next],
                        axis=0).astype(BF16)
    rows = tm + 2 * HALO

    def conv(u, cw):
        mid = slice(HALO, HALO + tm)
        return (pltpu.roll(u, 1, 0)[mid] * cw[0:1] + u[mid] * cw[1:2]
                + pltpu.roll(u, rows - 1, 0)[mid] * cw[2:3] + cw[3:4])

    def body(c, acc):
        a = conv(dot(h, wa_ref[c]), cwa_ref[c])
        b = conv(dot(h, wb_ref[c]), cwb_ref[c])
        act = (a * jax.nn.sigmoid(a) * b).astype(BF16)
        return acc + dot(act, wd_ref[c])

    y = lax.fori_loop(0, n_chunks, body, jnp.zeros((tm, x.shape[1]), F32))
    out = x + mod[5:6] * y
    if final:
        out = _rms(out, gfin_ref[...])
    o_ref[0] = out


def _ffn(x, mod_l, mod_row, g_ffn, w, tm, g_final=None):
    B, S, D = x.shape
    nc, _, fc = w["w_up_a"].shape
    const = lambda shape: pl.BlockSpec(shape, lambda b, i: (0,) * len(shape))
    r = tm // HALO
    last = S // HALO - 1
    ins = [x, x, x, mod_l, g_ffn, w["w_up_a"], w["w_up_b"], w["conv_a"], w["conv_b"], w["w_down"]]
    specs = [pl.BlockSpec((1, HALO, D), lambda b, i: (b, jnp.maximum(i * r - 1, 0), 0)),
             pl.BlockSpec((1, tm, D), lambda b, i: (b, i, 0)),
             pl.BlockSpec((1, HALO, D), lambda b, i: (b, jnp.minimum((i + 1) * r, last), 0)),
             pl.BlockSpec((1, 6, D), lambda b, i: (mod_row(b), 0, 0)),
             const((1, D)), const((nc, D, fc)), const((nc, D, fc)),
             const((nc, CONV_W + 1, fc)), const((nc, CONV_W + 1, fc)), const((nc, fc, D))]
    if g_final is not None:
        ins.append(g_final)
        specs.append(const((1, D)))
    return pl.pallas_call(
        functools.partial(_ffn_kernel, tm=tm, n_chunks=nc, final=g_final is not None),
        grid=(B, S // tm),
        in_specs=specs,
        out_specs=pl.BlockSpec((1, tm, D), lambda b, i: (b, i, 0)),
        out_shape=jax.ShapeDtypeStruct((B, S, D), F32),
        compiler_params=_params(2),
        name="ffn",
    )(*ins)


def _slot_pad(w, heads, width, left=0):
    k = w.shape[0]
    w = w.reshape(k, heads, width)
    w = jnp.pad(w, ((0, 0), (0, 0), (left, LANES - width - left)))
    return w.reshape(k, heads * LANES)


def _layer_weights(l, w_in, g_q, w_uq, g_kv, w_ukv, w_br_a, w_br_b, w_out, w_up, conv_w, conv_b, w_down, fc):
    D, F = D_MODEL, D_FF
    cuts = [0, Q_LORA, Q_LORA + KV_LORA, Q_LORA + KV_LORA + ROPE_A]
    wi = w_in[l]
    kr = jnp.pad(wi[:, cuts[2]:cuts[3]], ((0, 0), (NOPE_A, LANES - NOPE_A - ROPE_A)))
    w_in_p = jnp.concatenate([wi[:, :cuts[2]], kr, wi[:, cuts[3]:]], axis=1)
    ukv = w_ukv[l].reshape(KV_LORA, H_A, NOPE_A + V_A)
    kn, va = ukv[..., :NOPE_A], ukv[..., NOPE_A:]
    odd = (jnp.arange(H_A) % 2 == 1)[None, :, None]
    v_slots = jnp.where(odd, jnp.pad(va, ((0, 0), (0, 0), (LANES - V_A, 0))),
                        jnp.pad(va, ((0, 0), (0, 0), (0, LANES - V_A))))
    nc = F // fc
    chunks = lambda w: w.reshape(w.shape[0], nc, fc).swapaxes(0, 1)
    cw = jnp.concatenate([conv_w[l], conv_b[l][None]], axis=0)
    return {
        "w_in": w_in_p.astype(BF16),
        "g_q": g_q[l][None], "g_kv": g_kv[l][None],
        "w_uq": _slot_pad(w_uq[l], H_A, NOPE_A + ROPE_A).astype(BF16),
        "w_k": _slot_pad(kn.reshape(KV_LORA, H_A * NOPE_A), H_A, NOPE_A).astype(BF16),
        "w_v": v_slots.reshape(KV_LORA, H_A * LANES).astype(BF16),
        "w_br_a": w_br_a[l].astype(BF16), "w_br_b": w_br_b[l].astype(BF16), "w_out": w_out[l].astype(BF16),
        "w_up_a": chunks(w_up[l][:, :F]).astype(BF16), "w_up_b": chunks(w_up[l][:, F:]).astype(BF16),
        "conv_a": chunks(cw[:, :F]), "conv_b": chunks(cw[:, F:]),
        "w_down": w_down[l].reshape(nc, fc, D).astype(BF16),
    }


def _rope_tables(S, rotary):
    def slot_tables(dr, left, reps):
        q = dr // 4
        if rotary:
            t = jnp.arange(S)
            rows = (t // GRID_W).astype(F32)
            cols = (t % GRID_W).astype(F32)
            freqs = ROPE_BASE ** (-jnp.arange(q, dtype=F32) / q)
            ar, ac = rows[:, None] * freqs, cols[:, None] * freqs
            cos = jnp.concatenate([jnp.cos(ar), jnp.cos(ar), jnp.cos(ac), jnp.cos(ac)], axis=-1)
            sin = jnp.concatenate([-jnp.sin(ar), jnp.sin(ar), -jnp.sin(ac), jnp.sin(ac)], axis=-1)
        else:
            cos, sin = jnp.ones((S, dr), F32), jnp.zeros((S, dr), F32)
        cos, sin = jnp.tile(cos, (1, reps)), jnp.tile(sin, (1, reps))
        pad = ((0, 0), (left, LANES - left - dr * reps))
        return jnp.pad(cos, pad, constant_values=1.0), jnp.pad(sin, pad)

    ca, sa = slot_tables(ROPE_A, NOPE_A, 1)
    cb, sb = slot_tables(DH_B, 0, 2)
    scale_a = (NOPE_A + ROPE_A) ** -0.5 * math.log2(math.e)
    scale_b = DH_B ** -0.5 * math.log2(math.e)
    return jnp.concatenate([ca * scale_a, sa * scale_a, ca, sa, cb * scale_b, sb * scale_b, cb, sb], axis=1)


def kernel(x, c, ctx, c_ctx, w_ada, b_ada, g_mix, g_ffn, w_in, g_q, w_uq, g_kv, w_ukv, lam_q1, lam_k1, lam_q2,
           lam_k2, g_sub, w_br_a, w_br_b, w_out, w_up, conv_w, conv_b, w_down, g_final):
    B, S, D = x.shape
    C = ctx.shape[1]
    depth = w_in.shape[0]
    tm_proj, tm_merge, tm_ffn, tq, tk, fc = 256, 512, 512, 512, 512, 256
    tm_proj, tm_merge, tm_ffn, tq, tk = (min(t, S) for t in (tm_proj, tm_merge, tm_ffn, tq, tk))
    gate_dtype = F32

    ctx_row = B
    n_rows = -(-(B + 1) // 8) * 8
    cc = jnp.zeros((n_rows, D), F32).at[:B].set(c).at[ctx_row].set(c_ctx)
    mod = _ada(cc, w_ada, b_ada).reshape(depth, n_rows, 6, D)
    lat_row = lambda b: b
    ctx_mod_row = lambda b: ctx_row

    tab_lat = _rope_tables(S, True)
    tab_ctx = _rope_tables(C, False)

    x_lat, x_ctx = x, ctx
    for l in range(depth):
        last = l == depth - 1
        lam_init = 0.8 - 0.6 * math.exp(-0.3 * l)
        w = _layer_weights(l, w_in, g_q, w_uq, g_kv, w_ukv, w_br_a, w_br_b, w_out, w_up, conv_w, conv_b,
                           w_down, fc)
        lams = [v[l][None] for v in (lam_q1, lam_k1, lam_q2, lam_k2)]
        gs = g_sub[l][None]
        gm, gf = g_mix[l][None], g_ffn[l][None]

        pc = _proj(x_ctx, mod[l], ctx_mod_row, gm, w, tab_ctx, min(tm_proj, C), gate_dtype)
        pq = _proj(x_lat, mod[l], lat_row, gm, w, tab_lat, tm_proj, gate_dtype)
        qa, ka, va, qb, kb1, kb2, vb, sga, sgb = pq
        cqa, cka, cva, cqb, ckb1, ckb2, cvb, csga, csgb = pc

        o_a = _mla(qa, cka, cva, ka, va, tq=tq, tk=tk)
        o_b = _diff(lams, gs, qb, ckb1, ckb2, cvb, kb1, kb2, vb, tq=tq, tk=tk, lam_init=lam_init)
        if not last:
            oc_a = _mla(cqa, cka, cva, tq=C, tk=tk)
            oc_b = _diff(lams, gs, cqb, ckb1, ckb2, cvb, tq=C, tk=tk, lam_init=lam_init)
            x_ctx = _merge(x_ctx, mod[l], ctx_mod_row, oc_a, oc_b, csga, csgb, w, C)
            x_ctx = _ffn(x_ctx, mod[l], ctx_mod_row, gf, w, C)
        x_lat = _merge(x_lat, mod[l], lat_row, o_a, o_b, sga, sgb, w, tm_merge)
        x_lat = _ffn(x_lat, mod[l], lat_row, gf, w, tm_ffn, g_final[None] if last else None)
    return x_lat
```

```python
import functools
import math

import jax
import jax.numpy as jnp
from jax import lax
from jax.experimental import pallas as pl
from jax.experimental.pallas import tpu as pltpu

D_MODEL = 1024
GRID_W = 64
EPS = 1e-6
ROPE_BASE = 10000.0

H_A = 8
Q_LORA = 384
KV_LORA = 256
NOPE_A = 64
ROPE_A = 32
V_A = 64

H_B = 4
DH_B = 64
V_B = 2 * DH_B

D_FF = 2816
CONV_W = 3
QKV_B = H_B * 2 * DH_B

LANES = 128
HALO = 8
VMEM_LIMIT = 56 * 1024 * 1024

C_Q = 0
C_KV = C_Q + Q_LORA
C_KR = C_KV + KV_LORA
C_DQ = C_KR + LANES
C_DK = C_DQ + QKV_B
C_DV = C_DK + QKV_B
C_GA = C_DV + QKV_B
C_GB = C_GA + D_MODEL
IN_WP = C_GB + D_MODEL

BF16 = jnp.bfloat16
F32 = jnp.float32


def _params(n_grid):
    return pltpu.CompilerParams(dimension_semantics=("parallel",) * n_grid,
                                vmem_limit_bytes=VMEM_LIMIT)


def _rms(v, g):
    ms = jnp.mean(v * v, axis=-1, keepdims=True)
    return v * lax.rsqrt(ms + EPS) * g


def _swap_groups(v, q):
    n = v.shape[-1]
    lane = lax.broadcasted_iota(jnp.int32, v.shape, 1)
    ahead = pltpu.roll(v, n - q, 1)
    behind = pltpu.roll(v, q, 1)
    return jnp.where((lane & (2 * q - 1)) < q, ahead, behind)


def _rope(v, cos, sin, q):
    return v * cos + _swap_groups(v, q) * sin


def _ada_kernel(c_ref, w_ref, b_ref, o_ref):
    a = c_ref[...]
    a = a * jax.nn.sigmoid(a)
    w = w_ref[0]
    a_hi = a.astype(BF16)
    a_lo = (a - a_hi.astype(F32)).astype(BF16)
    w_hi = w.astype(BF16)
    w_lo = (w - w_hi.astype(F32)).astype(BF16)
    dot = functools.partial(jnp.dot, preferred_element_type=F32)
    o_ref[0] = dot(a_hi, w_hi) + dot(a_lo, w_hi) + dot(a_hi, w_lo) + b_ref[0]


def _ada(cc, w_ada, b_ada):
    L, D, N = w_ada.shape
    R = cc.shape[0]
    tn = 1024
    return pl.pallas_call(
        _ada_kernel,
        grid=(L, N // tn),
        in_specs=[pl.BlockSpec((R, D), lambda l, j: (0, 0)),
                  pl.BlockSpec((1, D, tn), lambda l, j: (l, 0, j)),
                  pl.BlockSpec((1, 1, tn), lambda l, j: (l, 0, j))],
        out_specs=pl.BlockSpec((1, R, tn), lambda l, j: (l, 0, j)),
        out_shape=jax.ShapeDtypeStruct((L, R, N), F32),
        compiler_params=_params(2),
        name="adaln",
    )(cc, w_ada, b_ada.reshape(L, 1, N))


def _proj_kernel(x_ref, mod_ref, gmix_ref, win_ref, gq_ref, wuq_ref, gkv_ref, wk_ref, wv_ref, tab_ref,
                 qa_ref, ka_ref, va_ref, qb_ref, kb1_ref, kb2_ref, vb_ref, sga_ref, sgb_ref):
    dot = functools.partial(jnp.dot, preferred_element_type=F32)
    x = x_ref[0]
    mod = mod_ref[0]
    h = (_rms(x, gmix_ref[...]) * (1.0 + mod[1:2]) + mod[0:1]).astype(BF16)
    tm = x.shape[0]
    lane = lax.broadcasted_iota(jnp.int32, (tm, LANES), 1)
    low = lane < (LANES // 2)
    tab = lambda k: tab_ref[:, k * LANES:(k + 1) * LANES]
    slot = lambda k: slice(k * LANES, (k + 1) * LANES)

    cq = dot(h, win_ref[:, C_Q:C_Q + Q_LORA])
    q = dot(_rms(cq, gq_ref[...]).astype(BF16), wuq_ref[...])
    for hd in range(H_A):
        qa_ref[0, :, slot(hd)] = _rope(q[:, slot(hd)], tab(0), tab(1), ROPE_A // 4).astype(BF16)
    ckv = _rms(dot(h, win_ref[:, C_KV:C_KV + KV_LORA]), gkv_ref[...]).astype(BF16)
    kn = dot(ckv, wk_ref[...])
    vv = dot(ckv, wv_ref[...])
    kr = _rope(dot(h, win_ref[:, C_KR:C_KR + LANES]), tab(2), tab(3), ROPE_A // 4)
    for hd in range(H_A):
        ka_ref[0, :, slot(hd)] = (kn[:, slot(hd)] + kr).astype(BF16)
        ones = jnp.where(low, 0.0, 1.0) if hd % 2 == 0 else jnp.where(low, 1.0, 0.0)
        va_ref[0, :, slot(hd)] = (vv[:, slot(hd)] + ones).astype(BF16)

    dq = dot(h, win_ref[:, C_DQ:C_DQ + QKV_B])
    dk = dot(h, win_ref[:, C_DK:C_DK + QKV_B])
    for hd in range(H_B):
        qb_ref[0, :, slot(hd)] = _rope(dq[:, slot(hd)], tab(4), tab(5), DH_B // 4).astype(BF16)
        k = _rope(dk[:, slot(hd)], tab(6), tab(7), DH_B // 4)
        kb1_ref[0, :, slot(hd)] = jnp.where(low, k, 0.0).astype(BF16)
        kb2_ref[0, :, slot(hd)] = jnp.where(low, 0.0, k).astype(BF16)
    vb_ref[0] = dot(h, win_ref[:, C_DV:C_DV + QKV_B]).astype(BF16)

    sga_ref[0] = jax.nn.sigmoid(dot(h, win_ref[:, C_GA:C_GA + D_MODEL])).astype(sga_ref.dtype)
    sgb_ref[0] = jax.nn.sigmoid(dot(h, win_ref[:, C_GB:C_GB + D_MODEL])).astype(sgb_ref.dtype)


def _proj(x, mod_l, mod_row, g_mix, w, tab, tm, gate_dtype):
    B, S, D = x.shape
    const = lambda shape: pl.BlockSpec(shape, lambda i, b: (0,) * len(shape))
    tok = lambda n: pl.BlockSpec((1, tm, n), lambda i, b: (b, i, 0))
    wa, wb = H_A * LANES, H_B * LANES
    outs = [(wa, BF16), (wa, BF16), (wa, BF16), (wb, BF16), (wb, BF16), (wb, BF16), (wb, BF16),
            (D, gate_dtype), (D, gate_dtype)]
    return pl.pallas_call(
        _proj_kernel,
        grid=(S // tm, B),
        in_specs=[tok(D),
                  pl.BlockSpec((1, 6, D), lambda i, b: (mod_row(b), 0, 0)),
                  const((1, D)), const((D, IN_WP)), const((1, Q_LORA)), const((Q_LORA, wa)),
                  const((1, KV_LORA)), const((KV_LORA, wa)), const((KV_LORA, wa)),
                  pl.BlockSpec((tm, 8 * LANES), lambda i, b: (i, 0))],
        out_specs=[tok(n) for n, _ in outs],
        out_shape=[jax.ShapeDtypeStruct((B, S, n), dt) for n, dt in outs],
        compiler_params=_params(2),
        name="proj",
    )(x, mod_l, g_mix, w["w_in"], w["g_q"], w["w_uq"], w["g_kv"], w["w_k"], w["w_v"], tab)


def _qk(q, k):
    return lax.dot_general(q, k, (((1,), (1,)), ((), ())), preferred_element_type=F32)


def _absorb(s, v, state, with_sum):
    smax = jnp.max(s, axis=-1, keepdims=True)
    m_new = smax if state is None else jnp.maximum(state[0], smax)
    p = jnp.exp2(s - m_new)
    pv = jnp.dot(p.astype(BF16), v, preferred_element_type=F32)
    if state is None:
        return (m_new, jnp.sum(p, axis=-1, keepdims=True), pv) if with_sum else (m_new, pv)
    alpha = jnp.exp2(state[0] - m_new)
    if with_sum:
        return m_new, alpha * state[1] + jnp.sum(p, axis=-1, keepdims=True), alpha * state[2] + pv
    return m_new, alpha * state[1] + pv


def _flash(n_streams, n_chunks, tk, score_ctx, score, value_ctx, value, with_sum, s_scr):
    streams = range(n_streams)
    states = tuple(_absorb(score_ctx(t), value_ctx(t), None, with_sum) for t in streams)
    if not n_chunks:
        return states
    rows = lambda i: pl.ds(i * tk, tk)
    for t in streams:
        s_scr[0, t] = score(t, rows(0))
    for i in range(n_chunks):
        slot, out = i % 2, []
        for t in streams:
            if i + 1 < n_chunks:
                s_scr[1 - slot, t] = score(t, rows(i + 1))
            out.append(_absorb(s_scr[slot, t], value(t, rows(i)), states[t], with_sum))
        states = tuple(out)
    return states


def _score_scratch(n_chunks, tq, tk):
    return pltpu.VMEM((2, 2, tq, tk) if n_chunks else (1, 1, 8, LANES), F32)


def _mla_kernel(*refs, n_chunks, tk):
    if n_chunks:
        q_ref, kc_ref, vc_ref, k_ref, v_ref, o_ref, s_scr = refs
    else:
        (q_ref, kc_ref, vc_ref, o_ref, s_scr), k_ref, v_ref = refs, None, None
    sl = lambda t: slice(t * LANES, (t + 1) * LANES)
    states = _flash(
        2, n_chunks, tk,
        score_ctx=lambda t: _qk(q_ref[0, :, sl(t)], kc_ref[0, :, sl(t)]),
        score=lambda t, rows: _qk(q_ref[0, :, sl(t)], k_ref[0, rows, sl(t)]),
        value_ctx=lambda t: vc_ref[0, :, sl(t)],
        value=lambda t, rows: v_ref[0, rows, sl(t)],
        with_sum=False, s_scr=s_scr)
    acc0, acc1 = states[0][1], states[1][1]
    lane = lax.broadcasted_iota(jnp.int32, acc0.shape, 1)
    o = jnp.where(lane < LANES // 2,
                  acc0 / pltpu.roll(acc0, LANES // 2, 1),
                  acc1 / pltpu.roll(acc1, LANES // 2, 1))
    o_ref[0] = o.astype(o_ref.dtype)


def _mla(q, kc, vc, k=None, v=None, *, tq, tk):
    B, Sq, _ = q.shape
    C = kc.shape[1]
    w = 2 * LANES
    qspec = pl.BlockSpec((1, tq, w), lambda b, h, i: (b, i, h))
    full = lambda n: pl.BlockSpec((1, n, w), lambda b, h, i: (b, 0, h))
    ins, specs, n_chunks = [q, kc, vc], [qspec, full(C), full(C)], 0
    if k is not None:
        S = k.shape[1]
        ins += [k, v]
        specs += [full(S), full(S)]
        n_chunks = S // tk
    return pl.pallas_call(
        functools.partial(_mla_kernel, n_chunks=n_chunks, tk=tk),
        grid=(B, H_A // 2, Sq // tq),
        in_specs=specs,
        out_specs=pl.BlockSpec((1, tq, LANES), lambda b, h, i: (b, i, h)),
        out_shape=jax.ShapeDtypeStruct((B, Sq, H_A * V_A), BF16),
        scratch_shapes=[_score_scratch(n_chunks, tq, tk)],
        compiler_params=_params(3),
        name="mla",
    )(*ins)


def _diff_kernel(*refs, n_chunks, tk, lam_init):
    lq1_ref, lk1_ref, lq2_ref, lk2_ref, gsub_ref, q_ref, k1c_ref, k2c_ref, vc_ref = refs[:9]
    if n_chunks:
        k1_ref, k2_ref, v_ref, o_ref, s_scr = refs[9:]
    else:
        (o_ref, s_scr), k1_ref, k2_ref, v_ref = refs[9:], None, None, None
    lam = (jnp.exp(jnp.sum(lq1_ref[...] * lk1_ref[...], axis=-1, keepdims=True))
           - jnp.exp(jnp.sum(lq2_ref[...] * lk2_ref[...], axis=-1, keepdims=True)) + lam_init)
    kc_refs, k_refs = (k1c_ref, k2c_ref), (k1_ref, k2_ref)
    (_, l1, o1), (_, l2, o2) = _flash(
        2, n_chunks, tk,
        score_ctx=lambda t: _qk(q_ref[0], kc_refs[t][0]),
        score=lambda t, rows: _qk(q_ref[0], k_refs[t][0, rows, :]),
        value_ctx=lambda t: vc_ref[0],
        value=lambda t, rows: v_ref[0, rows, :],
        with_sum=True, s_scr=s_scr)
    o = o1 / l1 - lam * (o2 / l2)
    o_ref[0] = (_rms(o, gsub_ref[...]) * (1.0 - lam_init)).astype(o_ref.dtype)


def _diff(lams, g_sub, q, k1c, k2c, vc, k1=None, k2=None, v=None, *, tq, tk, lam_init):
    B, Sq, _ = q.shape
    C = k1c.shape[1]
    small = lambda n: pl.BlockSpec((1, n), lambda b, h, i: (0, 0))
    full = lambda n: pl.BlockSpec((1, n, LANES), lambda b, h, i: (b, 0, h))
    tile = pl.BlockSpec((1, tq, LANES), lambda b, h, i: (b, i, h))
    ins = list(lams) + [g_sub, q, k1c, k2c, vc]
    specs = [small(DH_B)] * 4 + [small(V_B), tile, full(C), full(C), full(C)]
    n_chunks = 0
    if k1 is not None:
        S = k1.shape[1]
        ins += [k1, k2, v]
        specs += [full(S)] * 3
        n_chunks = S // tk
    return pl.pallas_call(
        functools.partial(_diff_kernel, n_chunks=n_chunks, tk=tk, lam_init=lam_init),
        grid=(B, H_B, Sq // tq),
        in_specs=specs,
        out_specs=tile,
        out_shape=jax.ShapeDtypeStruct((B, Sq, H_B * V_B), BF16),
        scratch_shapes=[_score_scratch(n_chunks, tq, tk)],
        compiler_params=_params(3),
        name="diff",
    )(*ins)


def _merge_kernel(x_ref, mod_ref, oa_ref, ob_ref, sga_ref, sgb_ref, wa_ref, wb_ref, wo_ref, o_ref):
    dot = functools.partial(jnp.dot, preferred_element_type=F32)
    ya = dot(oa_ref[0], wa_ref[...])
    yb = dot(ob_ref[0], wb_ref[...])
    merged = sga_ref[0].astype(F32) * ya + sgb_ref[0].astype(F32) * yb
    o_ref[0] = x_ref[0] + mod_ref[0][2:3] * dot(merged.astype(BF16), wo_ref[...])


def _merge(x, mod_l, mod_row, oa, ob, sga, sgb, w, tm):
    B, S, D = x.shape
    const = lambda shape: pl.BlockSpec(shape, lambda b, i: (0,) * len(shape))
    tok = lambda n: pl.BlockSpec((1, tm, n), lambda b, i: (b, i, 0))
    na, nb = H_A * V_A, H_B * V_B
    return pl.pallas_call(
        _merge_kernel,
        grid=(B, S // tm),
        in_specs=[tok(D), pl.BlockSpec((1, 6, D), lambda b, i: (mod_row(b), 0, 0)),
                  tok(na), tok(nb), tok(D), tok(D),
                  const((na, D)), const((nb, D)), const((D, D))],
        out_specs=tok(D),
        out_shape=jax.ShapeDtypeStruct((B, S, D), F32),
        compiler_params=_params(2),
        name="merge",
    )(x, mod_l, oa, ob, sga, sgb, w["w_br_a"], w["w_br_b"], w["w_out"])


def _ffn_kernel(*refs, tm, n_chunks, final):
    xp_ref, x_ref, xn_ref, mod_ref, g_ref, wa_ref, wb_ref, cwa_ref, cwb_ref, wd_ref = refs[:10]
    gfin_ref = refs[10] if final else None
    o_ref = refs[-1]
    dot = functools.partial(jnp.dot, preferred_element_type=F32)
    i = pl.program_id(1)
    mod = mod_ref[0]
    norm = lambda v: _rms(v, g_ref[...]) * (1.0 + mod[4:5]) + mod[3:4]
    x = x_ref[0]
    keep_prev = (i > 0).astype(F32)
    keep_next = (i < pl.num_programs(1) - 1).astype(F32)
    h = jnp.concatenate([norm(xp_ref[0]) * keep_prev, norm(x), norm(xn_ref[0]) * keep_next],
                        axis=0).astype(BF16)
    rows = tm + 2 * HALO

    def conv(u, cw):
        mid = slice(HALO, HALO + tm)
        return (pltpu.roll(u, 1, 0)[mid] * cw[0:1] + u[mid] * cw[1:2]
                + pltpu.roll(u, rows - 1, 0)[mid] * cw[2:3] + cw[3:4])

    up = lambda c: (dot(h, wa_ref[c]), dot(h, wb_ref[c]))
    ua, ub = up(0)
    y = None
    for c in range(n_chunks):
        nxt = up(c + 1) if c + 1 < n_chunks else None
        a = conv(ua, cwa_ref[c])
        b = conv(ub, cwb_ref[c])
        act = (a * jax.nn.sigmoid(a) * b).astype(BF16)
        yc = dot(act, wd_ref[c])
        y = yc if y is None else y + yc
        ua, ub = nxt if nxt is not None else (None, None)
    out = x + mod[5:6] * y
    if final:
        out = _rms(out, gfin_ref[...])
    o_ref[0] = out


def _ffn(x, mod_l, mod_row, g_ffn, w, tm, g_final=None):
    B, S, D = x.shape
    nc, _, fc = w["w_up_a"].shape
    const = lambda shape: pl.BlockSpec(shape, lambda b, i: (0,) * len(shape))
    r = tm // HALO
    last = S // HALO - 1
    ins = [x, x, x, mod_l, g_ffn, w["w_up_a"], w["w_up_b"], w["conv_a"], w["conv_b"], w["w_down"]]
    specs = [pl.BlockSpec((1, HALO, D), lambda b, i: (b, jnp.maximum(i * r - 1, 0), 0)),
             pl.BlockSpec((1, tm, D), lambda b, i: (b, i, 0)),
             pl.BlockSpec((1, HALO, D), lambda b, i: (b, jnp.minimum((i + 1) * r, last), 0)),
             pl.BlockSpec((1, 6, D), lambda b, i: (mod_row(b), 0, 0)),
             const((1, D)), const((nc, D, fc)), const((nc, D, fc)),
             const((nc, CONV_W + 1, fc)), const((nc, CONV_W + 1, fc)), const((nc, fc, D))]
    if g_final is not None:
        ins.append(g_final)
        specs.append(const((1, D)))
    return pl.pallas_call(
        functools.partial(_ffn_kernel, tm=tm, n_chunks=nc, final=g_final is not None),
        grid=(B, S // tm),
        in_specs=specs,
        out_specs=pl.BlockSpec((1, tm, D), lambda b, i: (b, i, 0)),
        out_shape=jax.ShapeDtypeStruct((B, S, D), F32),
        compiler_params=_params(2),
        name="ffn",
    )(*ins)


def _slot_pad(w, heads, width, left=0):
    k = w.shape[0]
    w = w.reshape(k, heads, width)
    w = jnp.pad(w, ((0, 0), (0, 0), (left, LANES - width - left)))
    return w.reshape(k, heads * LANES)


def _layer_weights(l, w_in, g_q, w_uq, g_kv, w_ukv, w_br_a, w_br_b, w_out, w_up, conv_w, conv_b, w_down, fc):
    D, F = D_MODEL, D_FF
    cuts = [0, Q_LORA, Q_LORA + KV_LORA, Q_LORA + KV_LORA + ROPE_A]
    wi = w_in[l]
    kr = jnp.pad(wi[:, cuts[2]:cuts[3]], ((0, 0), (NOPE_A, LANES - NOPE_A - ROPE_A)))
    w_in_p = jnp.concatenate([wi[:, :cuts[2]], kr, wi[:, cuts[3]:]], axis=1)
    ukv = w_ukv[l].reshape(KV_LORA, H_A, NOPE_A + V_A)
    kn, va = ukv[..., :NOPE_A], ukv[..., NOPE_A:]
    odd = (jnp.arange(H_A) % 2 == 1)[None, :, None]
    v_slots = jnp.where(odd, jnp.pad(va, ((0, 0), (0, 0), (LANES - V_A, 0))),
                        jnp.pad(va, ((0, 0), (0, 0), (0, LANES - V_A))))
    nc = F // fc
    chunks = lambda w: w.reshape(w.shape[0], nc, fc).swapaxes(0, 1)
    cw = jnp.concatenate([conv_w[l], conv_b[l][None]], axis=0)
    return {
        "w_in": w_in_p.astype(BF16),
        "g_q": g_q[l][None], "g_kv": g_kv[l][None],
        "w_uq": _slot_pad(w_uq[l], H_A, NOPE_A + ROPE_A).astype(BF16),
        "w_k": _slot_pad(kn.reshape(KV_LORA, H_A * NOPE_A), H_A, NOPE_A).astype(BF16),
        "w_v": v_slots.reshape(KV_LORA, H_A * LANES).astype(BF16),
        "w_br_a": w_br_a[l].astype(BF16), "w_br_b": w_br_b[l].astype(BF16), "w_out": w_out[l].astype(BF16),
        "w_up_a": chunks(w_up[l][:, :F]).astype(BF16), "w_up_b": chunks(w_up[l][:, F:]).astype(BF16),
        "conv_a": chunks(cw[:, :F]), "conv_b": chunks(cw[:, F:]),
        "w_down": w_down[l].reshape(nc, fc, D).astype(BF16),
    }


def _rope_tables(S, rotary):
    def slot_tables(dr, left, reps):
        q = dr // 4
        if rotary:
            t = jnp.arange(S)
            rows = (t // GRID_W).astype(F32)
            cols = (t % GRID_W).astype(F32)
            freqs = ROPE_BASE ** (-jnp.arange(q, dtype=F32) / q)
            ar, ac = rows[:, None] * freqs, cols[:, None] * freqs
            cos = jnp.concatenate([jnp.cos(ar), jnp.cos(ar), jnp.cos(ac), jnp.cos(ac)], axis=-1)
            sin = jnp.concatenate([-jnp.sin(ar), jnp.sin(ar), -jnp.sin(ac), jnp.sin(ac)], axis=-1)
        else:
            cos, sin = jnp.ones((S, dr), F32), jnp.zeros((S, dr), F32)
        cos, sin = jnp.tile(cos, (1, reps)), jnp.tile(sin, (1, reps))
        pad = ((0, 0), (left, LANES - left - dr * reps))
        return jnp.pad(cos, pad, constant_values=1.0), jnp.pad(sin, pad)

    ca, sa = slot_tables(ROPE_A, NOPE_A, 1)
    cb, sb = slot_tables(DH_B, 0, 2)
    scale_a = (NOPE_A + ROPE_A) ** -0.5 * math.log2(math.e)
    scale_b = DH_B ** -0.5 * math.log2(math.e)
    return jnp.concatenate([ca * scale_a, sa * scale_a, ca, sa, cb * scale_b, sb * scale_b, cb, sb], axis=1)


def kernel(x, c, ctx, c_ctx, w_ada, b_ada, g_mix, g_ffn, w_in, g_q, w_uq, g_kv, w_ukv, lam_q1, lam_k1, lam_q2,
           lam_k2, g_sub, w_br_a, w_br_b, w_out, w_up, conv_w, conv_b, w_down, g_final):
    B, S, D = x.shape
    C = ctx.shape[1]
    depth = w_in.shape[0]
    tm_proj, tm_merge, tm_ffn, tq, tk, fc = 256, 512, 512, 512, 512, 256
    tm_proj, tm_merge, tm_ffn, tq, tk = (min(t, S) for t in (tm_proj, tm_merge, tm_ffn, tq, tk))
    gate_dtype = F32

    ctx_row = B
    n_rows = -(-(B + 1) // 8) * 8
    cc = jnp.zeros((n_rows, D), F32).at[:B].set(c).at[ctx_row].set(c_ctx)
    mod = _ada(cc, w_ada, b_ada).reshape(depth, n_rows, 6, D)
    lat_row = lambda b: b
    ctx_mod_row = lambda b: ctx_row

    tab_lat = _rope_tables(S, True)
    tab_ctx = _rope_tables(C, False)

    x_lat, x_ctx = x, ctx
    for l in range(depth):
        last = l == depth - 1
        lam_init = 0.8 - 0.6 * math.exp(-0.3 * l)
        w = _layer_weights(l, w_in, g_q, w_uq, g_kv, w_ukv, w_br_a, w_br_b, w_out, w_up, conv_w, conv_b,
                           w_down, fc)
        lams = [v[l][None] for v in (lam_q1, lam_k1, lam_q2, lam_k2)]
        gs = g_sub[l][None]
        gm, gf = g_mix[l][None], g_ffn[l][None]

        pc = _proj(x_ctx, mod[l], ctx_mod_row, gm, w, tab_ctx, min(tm_proj, C), gate_dtype)
        pq = _proj(x_lat, mod[l], lat_row, gm, w, tab_lat, tm_proj, gate_dtype)
        qa, ka, va, qb, kb1, kb2, vb, sga, sgb = pq
        cqa, cka, cva, cqb, ckb1, ckb2, cvb, csga, csgb = pc

        o_a = _mla(qa, cka, cva, ka, va, tq=tq, tk=tk)
        o_b = _diff(lams, gs, qb, ckb1, ckb2, cvb, kb1, kb2, vb, tq=tq, tk=tk, lam_init=lam_init)
        if not last:
            oc_a = _mla(cqa, cka, cva, tq=C, tk=tk)
            oc_b = _diff(lams, gs, cqb, ckb1, ckb2, cvb, tq=C, tk=tk, lam_init=lam_init)
            x_ctx = _merge(x_ctx, mod[l], ctx_mod_row, oc_a, oc_b, csga, csgb, w, C)
            x_ctx = _ffn(x_ctx, mod[l], ctx_mod_row, gf, w, C)
        x_lat = _merge(x_lat, mod[l], lat_row, o_a, o_b, sga, sgb, w, tm_merge)
        x_lat = _ffn(x_lat, mod[l], lat_row, gf, w, tm_ffn, g_final[None] if last else None)
    return x_lat
```

```python
import functools
import math
from typing import NamedTuple

import jax
import jax.numpy as jnp
from jax import lax
from jax.experimental import pallas as pl
from jax.experimental.pallas import tpu as pltpu

D_MODEL = 1024
GRID_W = 64
EPS = 1e-6
ROPE_BASE = 10000.0

H_A = 8
Q_LORA = 384
KV_LORA = 256
NOPE_A = 64
ROPE_A = 32
V_A = 64

H_B = 4
DH_B = 64
V_B = 2 * DH_B

D_FF = 2816
CONV_W = 3
QKV_B = H_B * 2 * DH_B

LANES = 128
HALO = 8
VMEM_LIMIT = 56 * 1024 * 1024

C_Q = 0
C_KV = C_Q + Q_LORA
C_KR = C_KV + KV_LORA
C_DQ = C_KR + LANES
C_DK = C_DQ + QKV_B
C_DV = C_DK + QKV_B
C_GA = C_DV + QKV_B
C_GB = C_GA + D_MODEL
IN_WP = C_GB + D_MODEL

BF16 = jnp.bfloat16
F32 = jnp.float32


def _params(n_grid):
    return pltpu.CompilerParams(dimension_semantics=("parallel",) * n_grid,
                                vmem_limit_bytes=VMEM_LIMIT)


def _rms(v, g):
    ms = jnp.mean(v * v, axis=-1, keepdims=True)
    return v * lax.rsqrt(ms + EPS) * g


def _swap_groups(v, q):
    n = v.shape[-1]
    lane = lax.broadcasted_iota(jnp.int32, v.shape, 1)
    ahead = pltpu.roll(v, n - q, 1)
    behind = pltpu.roll(v, q, 1)
    return jnp.where((lane & (2 * q - 1)) < q, ahead, behind)


def _rope(v, cos, sin, q):
    return v * cos + _swap_groups(v, q) * sin


def _ada_kernel(c_ref, w_ref, b_ref, o_ref):
    a = c_ref[...]
    a = a * jax.nn.sigmoid(a)
    w = w_ref[0]
    a_hi = a.astype(BF16)
    a_lo = (a - a_hi.astype(F32)).astype(BF16)
    w_hi = w.astype(BF16)
    w_lo = (w - w_hi.astype(F32)).astype(BF16)
    dot = functools.partial(jnp.dot, preferred_element_type=F32)
    o_ref[0] = dot(a_hi, w_hi) + dot(a_lo, w_hi) + dot(a_hi, w_lo) + b_ref[0]


def _ada(cc, w_ada, b_ada):
    L, D, N = w_ada.shape
    R = cc.shape[0]
    tn = 1024
    return pl.pallas_call(
        _ada_kernel,
        grid=(L, N // tn),
        in_specs=[pl.BlockSpec((R, D), lambda l, j: (0, 0)),
                  pl.BlockSpec((1, D, tn), lambda l, j: (l, 0, j)),
                  pl.BlockSpec((1, 1, tn), lambda l, j: (l, 0, j))],
        out_specs=pl.BlockSpec((1, R, tn), lambda l, j: (l, 0, j)),
        out_shape=jax.ShapeDtypeStruct((L, R, N), F32),
        compiler_params=_params(2),
        name="adaln",
    )(cc, w_ada, b_ada.reshape(L, 1, N))


def _proj_kernel(x_ref, mod_ref, gmix_ref, win_ref, gq_ref, wuq_ref, gkv_ref, wk_ref, wv_ref, tab_ref,
                 qa_ref, ka_ref, va_ref, qb_ref, kb1_ref, kb2_ref, vb_ref, sga_ref, sgb_ref):
    dot = functools.partial(jnp.dot, preferred_element_type=F32)
    x = x_ref[0]
    mod = mod_ref[0]
    h = (_rms(x, gmix_ref[...]) * (1.0 + mod[1:2]) + mod[0:1]).astype(BF16)
    tm = x.shape[0]
    lane = lax.broadcasted_iota(jnp.int32, (tm, LANES), 1)
    low = lane < (LANES // 2)
    tab = lambda k: tab_ref[:, k * LANES:(k + 1) * LANES]
    slot = lambda k: slice(k * LANES, (k + 1) * LANES)

    cq = dot(h, win_ref[:, C_Q:C_Q + Q_LORA])
    q = dot(_rms(cq, gq_ref[...]).astype(BF16), wuq_ref[...])
    for hd in range(H_A):
        qa_ref[0, :, slot(hd)] = _rope(q[:, slot(hd)], tab(0), tab(1), ROPE_A // 4).astype(BF16)
    ckv = _rms(dot(h, win_ref[:, C_KV:C_KV + KV_LORA]), gkv_ref[...]).astype(BF16)
    kn = dot(ckv, wk_ref[...])
    vv = dot(ckv, wv_ref[...])
    kr = _rope(dot(h, win_ref[:, C_KR:C_KR + LANES]), tab(2), tab(3), ROPE_A // 4)
    for hd in range(H_A):
        ka_ref[0, :, slot(hd)] = (kn[:, slot(hd)] + kr).astype(BF16)
        ones = jnp.where(low, 0.0, 1.0) if hd % 2 == 0 else jnp.where(low, 1.0, 0.0)
        va_ref[0, :, slot(hd)] = (vv[:, slot(hd)] + ones).astype(BF16)

    dq = dot(h, win_ref[:, C_DQ:C_DQ + QKV_B])
    dk = dot(h, win_ref[:, C_DK:C_DK + QKV_B])
    for hd in range(H_B):
        qb_ref[0, :, slot(hd)] = _rope(dq[:, slot(hd)], tab(4), tab(5), DH_B // 4).astype(BF16)
        k = _rope(dk[:, slot(hd)], tab(6), tab(7), DH_B // 4)
        kb1_ref[0, :, slot(hd)] = jnp.where(low, k, 0.0).astype(BF16)
        kb2_ref[0, :, slot(hd)] = jnp.where(low, 0.0, k).astype(BF16)
    vb_ref[0] = dot(h, win_ref[:, C_DV:C_DV + QKV_B]).astype(BF16)

    sga_ref[0] = jax.nn.sigmoid(dot(h, win_ref[:, C_GA:C_GA + D_MODEL])).astype(sga_ref.dtype)
    sgb_ref[0] = jax.nn.sigmoid(dot(h, win_ref[:, C_GB:C_GB + D_MODEL])).astype(sgb_ref.dtype)


def _proj(x, mod_l, mod_row, g_mix, w, tab, tm):
    B, S, D = x.shape
    const = lambda shape: pl.BlockSpec(shape, lambda i, b: (0,) * len(shape))
    tok = lambda n: pl.BlockSpec((1, tm, n), lambda i, b: (b, i, 0))
    wa, wb = H_A * LANES, H_B * LANES
    outs = [(wa, BF16), (wa, BF16), (wa, BF16), (wb, BF16), (wb, BF16), (wb, BF16), (wb, BF16),
            (D, BF16), (D, BF16)]
    return pl.pallas_call(
        _proj_kernel,
        grid=(S // tm, B),
        in_specs=[tok(D),
                  pl.BlockSpec((1, 6, D), lambda i, b: (mod_row(b), 0, 0)),
                  const((1, D)), const((D, IN_WP)), const((1, Q_LORA)), const((Q_LORA, wa)),
                  const((1, KV_LORA)), const((KV_LORA, wa)), const((KV_LORA, wa)),
                  pl.BlockSpec((tm, 8 * LANES), lambda i, b: (i, 0))],
        out_specs=[tok(n) for n, _ in outs],
        out_shape=[jax.ShapeDtypeStruct((B, S, n), dt) for n, dt in outs],
        compiler_params=_params(2),
        name="proj",
    )(x, mod_l, g_mix, w["w_in"], w["g_q"], w["w_uq"], w["g_kv"], w["w_k"], w["w_v"], tab)


def _qk(q, k):
    return lax.dot_general(q, k, (((1,), (1,)), ((), ())), preferred_element_type=F32)


def _absorb(s, v, state):
    smax = jnp.max(s, axis=-1, keepdims=True)
    m_new = smax if state is None else jnp.maximum(state[0], smax)
    pv = jnp.dot(jnp.exp2(s - m_new).astype(BF16), v, preferred_element_type=F32)
    if state is None:
        return m_new, pv
    return m_new, jnp.exp2(state[0] - m_new) * state[1] + pv


def _flash(n_streams, n_chunks, tk, score_ctx, score, value_ctx, value, s_scr):
    streams = range(n_streams)
    states = tuple(_absorb(score_ctx(t), value_ctx(t), None) for t in streams)
    if not n_chunks:
        return states
    rows = lambda i: pl.ds(i * tk, tk)
    for t in streams:
        s_scr[0, t] = score(t, rows(0))
    for i in range(n_chunks):
        slot, out = i % 2, []
        for t in streams:
            if i + 1 < n_chunks:
                s_scr[1 - slot, t] = score(t, rows(i + 1))
            out.append(_absorb(s_scr[slot, t], value(t, rows(i)), states[t]))
        states = tuple(out)
    return states


def _score_scratch(n_chunks, tq, tk):
    return pltpu.VMEM((2, 2, tq, tk) if n_chunks else (1, 1, 8, LANES), F32)


def _mla_kernel(*refs, n_chunks, tk):
    if n_chunks:
        q_ref, kc_ref, vc_ref, k_ref, v_ref, o_ref, s_scr = refs
    else:
        (q_ref, kc_ref, vc_ref, o_ref, s_scr), k_ref, v_ref = refs, None, None
    sl = lambda t: slice(t * LANES, (t + 1) * LANES)
    states = _flash(
        2, n_chunks, tk,
        score_ctx=lambda t: _qk(q_ref[0, :, sl(t)], kc_ref[0, :, sl(t)]),
        score=lambda t, rows: _qk(q_ref[0, :, sl(t)], k_ref[0, rows, sl(t)]),
        value_ctx=lambda t: vc_ref[0, :, sl(t)],
        value=lambda t, rows: v_ref[0, rows, sl(t)],
        s_scr=s_scr)
    acc0, acc1 = states[0][1], states[1][1]
    lane = lax.broadcasted_iota(jnp.int32, acc0.shape, 1)
    o = jnp.where(lane < LANES // 2,
                  acc0 / pltpu.roll(acc0, LANES // 2, 1),
                  acc1 / pltpu.roll(acc1, LANES // 2, 1))
    o_ref[0] = o.astype(o_ref.dtype)


def _mla(q, kc, vc, k=None, v=None, *, tq, tk):
    B, Sq, _ = q.shape
    C = kc.shape[1]
    w = 2 * LANES
    qspec = pl.BlockSpec((1, tq, w), lambda b, h, i: (b, i, h))
    full = lambda n: pl.BlockSpec((1, n, w), lambda b, h, i: (b, 0, h))
    ins, specs, n_chunks = [q, kc, vc], [qspec, full(C), full(C)], 0
    if k is not None:
        S = k.shape[1]
        ins += [k, v]
        specs += [full(S), full(S)]
        n_chunks = S // tk
    return pl.pallas_call(
        functools.partial(_mla_kernel, n_chunks=n_chunks, tk=tk),
        grid=(B, H_A // 2, Sq // tq),
        in_specs=specs,
        out_specs=pl.BlockSpec((1, tq, LANES), lambda b, h, i: (b, i, h)),
        out_shape=jax.ShapeDtypeStruct((B, Sq, H_A * V_A), BF16),
        scratch_shapes=[_score_scratch(n_chunks, tq, tk)],
        compiler_params=_params(3),
        name="mla",
    )(*ins)


def _diff_kernel(*refs, n_chunks, tk, lam_init):
    lq1_ref, lk1_ref, lq2_ref, lk2_ref, gsub_ref, q_ref, k1c_ref, k2c_ref, vc_ref = refs[:9]
    if n_chunks:
        k1_ref, k2_ref, v_ref, o_ref, s_scr = refs[9:]
    else:
        (o_ref, s_scr), k1_ref, k2_ref, v_ref = refs[9:], None, None, None
    lam = (jnp.exp(jnp.sum(lq1_ref[...] * lk1_ref[...], axis=-1, keepdims=True))
           - jnp.exp(jnp.sum(lq2_ref[...] * lk2_ref[...], axis=-1, keepdims=True)) + lam_init)
    kc_refs, k_refs = (k1c_ref, k2c_ref), (k1_ref, k2_ref)
    ext = lambda v: jnp.concatenate([v, jnp.ones_like(v)], axis=1)
    (_, a1), (_, a2) = _flash(
        2, n_chunks, tk,
        score_ctx=lambda t: _qk(q_ref[0], kc_refs[t][0]),
        score=lambda t, rows: _qk(q_ref[0], k_refs[t][0, rows, :]),
        value_ctx=lambda t: ext(vc_ref[0]),
        value=lambda t, rows: ext(v_ref[0, rows, :]),
        s_scr=s_scr)
    o = a1[:, :LANES] / a1[:, LANES:] - lam * (a2[:, :LANES] / a2[:, LANES:])
    o_ref[0] = (_rms(o, gsub_ref[...]) * (1.0 - lam_init)).astype(o_ref.dtype)


def _diff(lams, g_sub, q, k1c, k2c, vc, k1=None, k2=None, v=None, *, tq, tk, lam_init):
    B, Sq, _ = q.shape
    C = k1c.shape[1]
    small = lambda n: pl.BlockSpec((1, n), lambda b, h, i: (0, 0))
    full = lambda n: pl.BlockSpec((1, n, LANES), lambda b, h, i: (b, 0, h))
    tile = pl.BlockSpec((1, tq, LANES), lambda b, h, i: (b, i, h))
    ins = list(lams) + [g_sub, q, k1c, k2c, vc]
    specs = [small(DH_B)] * 4 + [small(V_B), tile, full(C), full(C), full(C)]
    n_chunks = 0
    if k1 is not None:
        S = k1.shape[1]
        ins += [k1, k2, v]
        specs += [full(S)] * 3
        n_chunks = S // tk
    return pl.pallas_call(
        functools.partial(_diff_kernel, n_chunks=n_chunks, tk=tk, lam_init=lam_init),
        grid=(B, H_B, Sq // tq),
        in_specs=specs,
        out_specs=tile,
        out_shape=jax.ShapeDtypeStruct((B, Sq, H_B * V_B), BF16),
        scratch_shapes=[_score_scratch(n_chunks, tq, tk)],
        compiler_params=_params(3),
        name="diff",
    )(*ins)


def _merge_kernel(x_ref, mod_ref, oa_ref, ob_ref, sga_ref, sgb_ref, wa_ref, wb_ref, wo_ref, o_ref):
    dot = functools.partial(jnp.dot, preferred_element_type=F32)
    ya = dot(oa_ref[0], wa_ref[...])
    yb = dot(ob_ref[0], wb_ref[...])
    merged = sga_ref[0].astype(F32) * ya + sgb_ref[0].astype(F32) * yb
    o_ref[0] = x_ref[0] + mod_ref[0][2:3] * dot(merged.astype(BF16), wo_ref[...])


def _merge(x, mod_l, mod_row, oa, ob, sga, sgb, w, tm):
    B, S, D = x.shape
    const = lambda shape: pl.BlockSpec(shape, lambda b, i: (0,) * len(shape))
    tok = lambda n: pl.BlockSpec((1, tm, n), lambda b, i: (b, i, 0))
    na, nb = H_A * V_A, H_B * V_B
    return pl.pallas_call(
        _merge_kernel,
        grid=(B, S // tm),
        in_specs=[tok(D), pl.BlockSpec((1, 6, D), lambda b, i: (mod_row(b), 0, 0)),
                  tok(na), tok(nb), tok(D), tok(D),
                  const((na, D)), const((nb, D)), const((D, D))],
        out_specs=tok(D),
        out_shape=jax.ShapeDtypeStruct((B, S, D), F32),
        compiler_params=_params(2),
        name="merge",
    )(x, mod_l, oa, ob, sga, sgb, w["w_br_a"], w["w_br_b"], w["w_out"])


def _ffn_kernel(*refs, tm, final):
    xp_ref, x_ref, xn_ref, mod_ref, g_ref, wa_ref, wb_ref, cwa_ref, cwb_ref, wd_ref = refs[:10]
    gfin_ref = refs[10] if final else None
    o_ref = refs[-1]
    dot = functools.partial(jnp.dot, preferred_element_type=F32)
    i = pl.program_id(1)
    mod = mod_ref[0]
    norm = lambda v: _rms(v, g_ref[...]) * (1.0 + mod[4:5]) + mod[3:4]
    x = x_ref[0]
    keep_prev = (i > 0).astype(F32)
    keep_next = (i < pl.num_programs(1) - 1).astype(F32)
    h = jnp.concatenate([norm(xp_ref[0]) * keep_prev, norm(x), norm(xn_ref[0]) * keep_next],
                        axis=0).astype(BF16)
    rows = tm + 2 * HALO

    def conv(u, cw):
        mid = slice(HALO, HALO + tm)
        return (pltpu.roll(u, 1, 0)[mid] * cw[0:1] + u[mid] * cw[1:2]
                + pltpu.roll(u, rows - 1, 0)[mid] * cw[2:3] + cw[3:4])

    a = conv(dot(h, wa_ref[...]), cwa_ref[...])
    b = conv(dot(h, wb_ref[...]), cwb_ref[...])
    act = (a * jax.nn.sigmoid(a) * b).astype(BF16)
    out = x + mod[5:6] * dot(act, wd_ref[...])
    if final:
        out = _rms(out, gfin_ref[...])
    o_ref[0] = out


def _ffn(x, mod_l, mod_row, g_ffn, w, tm, g_final=None):
    B, S, D = x.shape
    F = w["w_down"].shape[0]
    const = lambda shape: pl.BlockSpec(shape, lambda b, i: (0,) * len(shape))
    r = tm // HALO
    last = S // HALO - 1
    ins = [x, x, x, mod_l, g_ffn, w["w_up_a"], w["w_up_b"], w["conv_a"], w["conv_b"], w["w_down"]]
    specs = [pl.BlockSpec((1, HALO, D), lambda b, i: (b, jnp.maximum(i * r - 1, 0), 0)),
             pl.BlockSpec((1, tm, D), lambda b, i: (b, i, 0)),
             pl.BlockSpec((1, HALO, D), lambda b, i: (b, jnp.minimum((i + 1) * r, last), 0)),
             pl.BlockSpec((1, 6, D), lambda b, i: (mod_row(b), 0, 0)),
             const((1, D)), const((D, F)), const((D, F)),
             const((CONV_W + 1, F)), const((CONV_W + 1, F)), const((F, D))]
    if g_final is not None:
        ins.append(g_final)
        specs.append(const((1, D)))
    return pl.pallas_call(
        functools.partial(_ffn_kernel, tm=tm, final=g_final is not None),
        grid=(B, S // tm),
        in_specs=specs,
        out_specs=pl.BlockSpec((1, tm, D), lambda b, i: (b, i, 0)),
        out_shape=jax.ShapeDtypeStruct((B, S, D), F32),
        compiler_params=_params(2),
        name="ffn",
    )(*ins)


def _slot_pad(w, heads, width, left=0):
    k = w.shape[0]
    w = w.reshape(k, heads, width)
    w = jnp.pad(w, ((0, 0), (0, 0), (left, LANES - width - left)))
    return w.reshape(k, heads * LANES)


def _layer_weights(l, w_in, g_q, w_uq, g_kv, w_ukv, w_br_a, w_br_b, w_out, w_up, conv_w, conv_b, w_down):
    D, F = D_MODEL, D_FF
    cuts = [0, Q_LORA, Q_LORA + KV_LORA, Q_LORA + KV_LORA + ROPE_A]
    wi = w_in[l]
    kr = jnp.pad(wi[:, cuts[2]:cuts[3]], ((0, 0), (NOPE_A, LANES - NOPE_A - ROPE_A)))
    w_in_p = jnp.concatenate([wi[:, :cuts[2]], kr, wi[:, cuts[3]:]], axis=1)
    ukv = w_ukv[l].reshape(KV_LORA, H_A, NOPE_A + V_A)
    kn, va = ukv[..., :NOPE_A], ukv[..., NOPE_A:]
    odd = (jnp.arange(H_A) % 2 == 1)[None, :, None]
    v_slots = jnp.where(odd, jnp.pad(va, ((0, 0), (0, 0), (LANES - V_A, 0))),
                        jnp.pad(va, ((0, 0), (0, 0), (0, LANES - V_A))))
    cw = jnp.concatenate([conv_w[l], conv_b[l][None]], axis=0)
    return {
        "w_in": w_in_p.astype(BF16),
        "g_q": g_q[l][None], "g_kv": g_kv[l][None],
        "w_uq": _slot_pad(w_uq[l], H_A, NOPE_A + ROPE_A).astype(BF16),
        "w_k": _slot_pad(kn.reshape(KV_LORA, H_A * NOPE_A), H_A, NOPE_A).astype(BF16),
        "w_v": v_slots.reshape(KV_LORA, H_A * LANES).astype(BF16),
        "w_br_a": w_br_a[l].astype(BF16), "w_br_b": w_br_b[l].astype(BF16), "w_out": w_out[l].astype(BF16),
        "w_up_a": w_up[l][:, :F].astype(BF16), "w_up_b": w_up[l][:, F:].astype(BF16),
        "conv_a": cw[:, :F], "conv_b": cw[:, F:],
        "w_down": w_down[l].astype(BF16),
    }


def _rope_tables(S, rotary):
    def slot_tables(dr, left, reps):
        q = dr // 4
        if rotary:
            t = jnp.arange(S)
            rows = (t // GRID_W).astype(F32)
            cols = (t % GRID_W).astype(F32)
            freqs = ROPE_BASE ** (-jnp.arange(q, dtype=F32) / q)
            ar, ac = rows[:, None] * freqs, cols[:, None] * freqs
            cos = jnp.concatenate([jnp.cos(ar), jnp.cos(ar), jnp.cos(ac), jnp.cos(ac)], axis=-1)
            sin = jnp.concatenate([-jnp.sin(ar), jnp.sin(ar), -jnp.sin(ac), jnp.sin(ac)], axis=-1)
        else:
            cos, sin = jnp.ones((S, dr), F32), jnp.zeros((S, dr), F32)
        cos, sin = jnp.tile(cos, (1, reps)), jnp.tile(sin, (1, reps))
        pad = ((0, 0), (left, LANES - left - dr * reps))
        return jnp.pad(cos, pad, constant_values=1.0), jnp.pad(sin, pad)

    ca, sa = slot_tables(ROPE_A, NOPE_A, 1)
    cb, sb = slot_tables(DH_B, 0, 2)
    scale_a = (NOPE_A + ROPE_A) ** -0.5 * math.log2(math.e)
    scale_b = DH_B ** -0.5 * math.log2(math.e)
    return jnp.concatenate([ca * scale_a, sa * scale_a, ca, sa, cb * scale_b, sb * scale_b, cb, sb], axis=1)


class _Tiles(NamedTuple):
    proj: int
    merge: int
    ffn: int
    mla_q: int
    mla_k: int
    diff_q: int
    diff_k: int


def _tiles(S):
    return _Tiles(*(min(v, S) for v in (256, 512, 512, 1024, 1024, 512, 1024)))


def kernel(x, c, ctx, c_ctx, w_ada, b_ada, g_mix, g_ffn, w_in, g_q, w_uq, g_kv, w_ukv, lam_q1, lam_k1, lam_q2,
           lam_k2, g_sub, w_br_a, w_br_b, w_out, w_up, conv_w, conv_b, w_down, g_final):
    B, S, D = x.shape
    C = ctx.shape[1]
    depth = w_in.shape[0]
    t = _tiles(S)

    ctx_row = B
    n_rows = -(-(B + 1) // 8) * 8
    cc = jnp.zeros((n_rows, D), F32).at[:B].set(c).at[ctx_row].set(c_ctx)
    mod = _ada(cc, w_ada, b_ada).reshape(depth, n_rows, 6, D)
    lat_row = lambda b: b
    ctx_mod_row = lambda b: ctx_row

    tab_lat = _rope_tables(S, True)
    tab_ctx = _rope_tables(C, False)

    x_lat, x_ctx = x, ctx
    for l in range(depth):
        last = l == depth - 1
        lam_init = 0.8 - 0.6 * math.exp(-0.3 * l)
        w = _layer_weights(l, w_in, g_q, w_uq, g_kv, w_ukv, w_br_a, w_br_b, w_out, w_up, conv_w, conv_b,
                           w_down)
        lams = [v[l][None] for v in (lam_q1, lam_k1, lam_q2, lam_k2)]
        gs = g_sub[l][None]
        gm, gf = g_mix[l][None], g_ffn[l][None]

        pc = _proj(x_ctx, mod[l], ctx_mod_row, gm, w, tab_ctx, min(t.proj, C))
        pq = _proj(x_lat, mod[l], lat_row, gm, w, tab_lat, t.proj)
        qa, ka, va, qb, kb1, kb2, vb, sga, sgb = pq
        cqa, cka, cva, cqb, ckb1, ckb2, cvb, csga, csgb = pc

        o_a = _mla(qa, cka, cva, ka, va, tq=t.mla_q, tk=t.mla_k)
        o_b = _diff(lams, gs, qb, ckb1, ckb2, cvb, kb1, kb2, vb, tq=t.diff_q, tk=t.diff_k, lam_init=lam_init)
        if not last:
            oc_a = _mla(cqa, cka, cva, tq=C, tk=C)
            oc_b = _diff(lams, gs, cqb, ckb1, ckb2, cvb, tq=C, tk=C, lam_init=lam_init)
            x_ctx = _merge(x_ctx, mod[l], ctx_mod_row, oc_a, oc_b, csga, csgb, w, C)
            x_ctx = _ffn(x_ctx, mod[l], ctx_mod_row, gf, w, C)
        x_lat = _merge(x_lat, mod[l], lat_row, o_a, o_b, sga, sgb, w, t.merge)
        x_lat = _ffn(x_lat, mod[l], lat_row, gf, w, t.ffn, g_final[None] if last else None)
    return x_lat
```

```python
import functools
import math
from typing import NamedTuple

import jax
import jax.numpy as jnp
from jax import lax
from jax.experimental import pallas as pl
from jax.experimental.pallas import tpu as pltpu

D_MODEL = 1024
GRID_W = 64
EPS = 1e-6
ROPE_BASE = 10000.0

H_A = 8
Q_LORA = 384
KV_LORA = 256
NOPE_A = 64
ROPE_A = 32
V_A = 64

H_B = 4
DH_B = 64
V_B = 2 * DH_B

D_FF = 2816
CONV_W = 3
QKV_B = H_B * 2 * DH_B

LANES = 128
HALO = 8
VMEM_LIMIT = 56 * 1024 * 1024

C_Q = 0
C_KV = C_Q + Q_LORA
C_KR = C_KV + KV_LORA
C_DQ = C_KR + LANES
C_DK = C_DQ + QKV_B
C_DV = C_DK + QKV_B
C_GA = C_DV + QKV_B
C_GB = C_GA + D_MODEL
IN_WP = C_GB + D_MODEL

BF16 = jnp.bfloat16
F32 = jnp.float32


def _params(n_grid):
    return pltpu.CompilerParams(dimension_semantics=("parallel",) * n_grid,
                                vmem_limit_bytes=VMEM_LIMIT)


def _rms(v, g):
    ms = jnp.mean(v * v, axis=-1, keepdims=True)
    return v * lax.rsqrt(ms + EPS) * g


def _swap_groups(v, q):
    n = v.shape[-1]
    lane = lax.broadcasted_iota(jnp.int32, v.shape, 1)
    ahead = pltpu.roll(v, n - q, 1)
    behind = pltpu.roll(v, q, 1)
    return jnp.where((lane & (2 * q - 1)) < q, ahead, behind)


def _rope(v, cos, sin, q):
    return v * cos + _swap_groups(v, q) * sin


def _ada_kernel(c_ref, w_ref, b_ref, o_ref):
    a = c_ref[...]
    a = a * jax.nn.sigmoid(a)
    w = w_ref[0]
    a_hi = a.astype(BF16)
    a_lo = (a - a_hi.astype(F32)).astype(BF16)
    w_hi = w.astype(BF16)
    w_lo = (w - w_hi.astype(F32)).astype(BF16)
    dot = functools.partial(jnp.dot, preferred_element_type=F32)
    o_ref[0] = dot(a_hi, w_hi) + dot(a_lo, w_hi) + dot(a_hi, w_lo) + b_ref[0]


def _ada(cc, w_ada, b_ada):
    L, D, N = w_ada.shape
    R = cc.shape[0]
    tn = 1024
    return pl.pallas_call(
        _ada_kernel,
        grid=(L, N // tn),
        in_specs=[pl.BlockSpec((R, D), lambda l, j: (0, 0)),
                  pl.BlockSpec((1, D, tn), lambda l, j: (l, 0, j)),
                  pl.BlockSpec((1, 1, tn), lambda l, j: (l, 0, j))],
        out_specs=pl.BlockSpec((1, R, tn), lambda l, j: (l, 0, j)),
        out_shape=jax.ShapeDtypeStruct((L, R, N), F32),
        compiler_params=_params(2),
        name="adaln",
    )(cc, w_ada, b_ada.reshape(L, 1, N))


def _proj_kernel(x_ref, mod_ref, gmix_ref, win_ref, gq_ref, wuq_ref, gkv_ref, wk_ref, wv_ref, tab_ref,
                 qa_ref, ka_ref, va_ref, qb_ref, kb1_ref, kb2_ref, vb_ref, sga_ref, sgb_ref):
    dot = functools.partial(jnp.dot, preferred_element_type=F32)
    x = x_ref[0]
    mod = mod_ref[0]
    h = (_rms(x, gmix_ref[...]) * (1.0 + mod[1:2]) + mod[0:1]).astype(BF16)
    tm = x.shape[0]
    lane = lax.broadcasted_iota(jnp.int32, (tm, LANES), 1)
    low = lane < (LANES // 2)
    tab = lambda k: tab_ref[:, k * LANES:(k + 1) * LANES]
    slot = lambda k: slice(k * LANES, (k + 1) * LANES)

    z = dot(h, win_ref[...])
    cq = z[:, C_Q:C_Q + Q_LORA]
    q = dot(_rms(cq, gq_ref[...]).astype(BF16), wuq_ref[...])
    for hd in range(H_A):
        qa_ref[0, :, slot(hd)] = _rope(q[:, slot(hd)], tab(0), tab(1), ROPE_A // 4).astype(BF16)
    ckv = _rms(z[:, C_KV:C_KV + KV_LORA], gkv_ref[...]).astype(BF16)
    kn = dot(ckv, wk_ref[...])
    vv = dot(ckv, wv_ref[...])
    kr = _rope(z[:, C_KR:C_KR + LANES], tab(2), tab(3), ROPE_A // 4)
    for hd in range(H_A):
        ka_ref[0, :, slot(hd)] = (kn[:, slot(hd)] + kr).astype(BF16)
        ones = jnp.where(low, 0.0, 1.0) if hd % 2 == 0 else jnp.where(low, 1.0, 0.0)
        va_ref[0, :, slot(hd)] = (vv[:, slot(hd)] + ones).astype(BF16)

    dq = z[:, C_DQ:C_DQ + QKV_B]
    dk = z[:, C_DK:C_DK + QKV_B]
    for hd in range(H_B):
        qb_ref[0, :, slot(hd)] = _rope(dq[:, slot(hd)], tab(4), tab(5), DH_B // 4).astype(BF16)
        k = _rope(dk[:, slot(hd)], tab(6), tab(7), DH_B // 4)
        kb1_ref[0, :, slot(hd)] = jnp.where(low, k, 0.0).astype(BF16)
        kb2_ref[0, :, slot(hd)] = jnp.where(low, 0.0, k).astype(BF16)
    vb_ref[0] = z[:, C_DV:C_DV + QKV_B].astype(BF16)

    sga_ref[0] = jax.nn.sigmoid(z[:, C_GA:C_GA + D_MODEL]).astype(sga_ref.dtype)
    sgb_ref[0] = jax.nn.sigmoid(z[:, C_GB:C_GB + D_MODEL]).astype(sgb_ref.dtype)


def _proj(x, mod_l, mod_row, g_mix, w, tab, tm):
    B, S, D = x.shape
    const = lambda shape: pl.BlockSpec(shape, lambda i, b: (0,) * len(shape))
    tok = lambda n: pl.BlockSpec((1, tm, n), lambda i, b: (b, i, 0))
    wa, wb = H_A * LANES, H_B * LANES
    outs = [(wa, BF16), (wa, BF16), (wa, BF16), (wb, BF16), (wb, BF16), (wb, BF16), (wb, BF16),
            (D, BF16), (D, BF16)]
    return pl.pallas_call(
        _proj_kernel,
        grid=(S // tm, B),
        in_specs=[tok(D),
                  pl.BlockSpec((1, 6, D), lambda i, b: (mod_row(b), 0, 0)),
                  const((1, D)), const((D, IN_WP)), const((1, Q_LORA)), const((Q_LORA, wa)),
                  const((1, KV_LORA)), const((KV_LORA, wa)), const((KV_LORA, wa)),
                  pl.BlockSpec((tm, 8 * LANES), lambda i, b: (i, 0))],
        out_specs=[tok(n) for n, _ in outs],
        out_shape=[jax.ShapeDtypeStruct((B, S, n), dt) for n, dt in outs],
        compiler_params=_params(2),
        name="proj",
    )(x, mod_l, g_mix, w["w_in"], w["g_q"], w["w_uq"], w["g_kv"], w["w_k"], w["w_v"], tab)


def _qk(q, k):
    return lax.dot_general(q, k, (((1,), (1,)), ((), ())), preferred_element_type=F32)


def _absorb(s, v, state):
    smax = jnp.max(s, axis=-1, keepdims=True)
    m_new = smax if state is None else jnp.maximum(state[0], smax)
    pv = jnp.dot(jnp.exp2(s - m_new).astype(BF16), v, preferred_element_type=F32)
    if state is None:
        return m_new, pv
    return m_new, jnp.exp2(state[0] - m_new) * state[1] + pv


def _flash(n_streams, n_chunks, tk, score_ctx, score, value_ctx, value, s_scr):
    streams = range(n_streams)
    states = tuple(_absorb(score_ctx(t), value_ctx(t), None) for t in streams)
    if not n_chunks:
        return states
    rows = lambda i: pl.ds(i * tk, tk)
    for t in streams:
        s_scr[0, t] = score(t, rows(0))
    for i in range(n_chunks):
        slot, out = i % 2, []
        for t in streams:
            if i + 1 < n_chunks:
                s_scr[1 - slot, t] = score(t, rows(i + 1))
            out.append(_absorb(s_scr[slot, t], value(t, rows(i)), states[t]))
        states = tuple(out)
    return states


def _score_scratch(n_chunks, tq, tk):
    return pltpu.VMEM((2, 2, tq, tk) if n_chunks else (1, 1, 8, LANES), F32)


def _mla_kernel(*refs, n_chunks, tk):
    if n_chunks:
        q_ref, kc_ref, vc_ref, k_ref, v_ref, o_ref, s_scr = refs
    else:
        (q_ref, kc_ref, vc_ref, o_ref, s_scr), k_ref, v_ref = refs, None, None
    sl = lambda t: slice(t * LANES, (t + 1) * LANES)
    states = _flash(
        2, n_chunks, tk,
        score_ctx=lambda t: _qk(q_ref[0, :, sl(t)], kc_ref[0, :, sl(t)]),
        score=lambda t, rows: _qk(q_ref[0, :, sl(t)], k_ref[0, rows, sl(t)]),
        value_ctx=lambda t: vc_ref[0, :, sl(t)],
        value=lambda t, rows: v_ref[0, rows, sl(t)],
        s_scr=s_scr)
    acc0, acc1 = states[0][1], states[1][1]
    lane = lax.broadcasted_iota(jnp.int32, acc0.shape, 1)
    o = jnp.where(lane < LANES // 2,
                  acc0 / pltpu.roll(acc0, LANES // 2, 1),
                  acc1 / pltpu.roll(acc1, LANES // 2, 1))
    o_ref[0] = o.astype(o_ref.dtype)


def _mla(q, kc, vc, k=None, v=None, *, tq, tk):
    B, Sq, _ = q.shape
    C = kc.shape[1]
    w = 2 * LANES
    qspec = pl.BlockSpec((1, tq, w), lambda b, h, i: (b, i, h))
    full = lambda n: pl.BlockSpec((1, n, w), lambda b, h, i: (b, 0, h))
    ins, specs, n_chunks = [q, kc, vc], [qspec, full(C), full(C)], 0
    if k is not None:
        S = k.shape[1]
        ins += [k, v]
        specs += [full(S), full(S)]
        n_chunks = S // tk
    return pl.pallas_call(
        functools.partial(_mla_kernel, n_chunks=n_chunks, tk=tk),
        grid=(B, H_A // 2, Sq // tq),
        in_specs=specs,
        out_specs=pl.BlockSpec((1, tq, LANES), lambda b, h, i: (b, i, h)),
        out_shape=jax.ShapeDtypeStruct((B, Sq, H_A * V_A), BF16),
        scratch_shapes=[_score_scratch(n_chunks, tq, tk)],
        compiler_params=_params(3),
        name="mla",
    )(*ins)


def _diff_kernel(*refs, n_chunks, tk, lam_init):
    lq1_ref, lk1_ref, lq2_ref, lk2_ref, gsub_ref, q_ref, k1c_ref, k2c_ref, vc_ref = refs[:9]
    if n_chunks:
        k1_ref, k2_ref, v_ref, o_ref, s_scr = refs[9:]
    else:
        (o_ref, s_scr), k1_ref, k2_ref, v_ref = refs[9:], None, None, None
    lam = (jnp.exp(jnp.sum(lq1_ref[...] * lk1_ref[...], axis=-1, keepdims=True))
           - jnp.exp(jnp.sum(lq2_ref[...] * lk2_ref[...], axis=-1, keepdims=True)) + lam_init)
    kc_refs, k_refs = (k1c_ref, k2c_ref), (k1_ref, k2_ref)
    ext = lambda v: jnp.concatenate([v, jnp.ones_like(v)], axis=1)
    (_, a1), (_, a2) = _flash(
        2, n_chunks, tk,
        score_ctx=lambda t: _qk(q_ref[0], kc_refs[t][0]),
        score=lambda t, rows: _qk(q_ref[0], k_refs[t][0, rows, :]),
        value_ctx=lambda t: ext(vc_ref[0]),
        value=lambda t, rows: ext(v_ref[0, rows, :]),
        s_scr=s_scr)
    o = a1[:, :LANES] / a1[:, LANES:] - lam * (a2[:, :LANES] / a2[:, LANES:])
    o_ref[0] = (_rms(o, gsub_ref[...]) * (1.0 - lam_init)).astype(o_ref.dtype)


def _diff(lams, g_sub, q, k1c, k2c, vc, k1=None, k2=None, v=None, *, tq, tk, lam_init):
    B, Sq, _ = q.shape
    C = k1c.shape[1]
    small = lambda n: pl.BlockSpec((1, n), lambda b, h, i: (0, 0))
    full = lambda n: pl.BlockSpec((1, n, LANES), lambda b, h, i: (b, 0, h))
    tile = pl.BlockSpec((1, tq, LANES), lambda b, h, i: (b, i, h))
    ins = list(lams) + [g_sub, q, k1c, k2c, vc]
    specs = [small(DH_B)] * 4 + [small(V_B), tile, full(C), full(C), full(C)]
    n_chunks = 0
    if k1 is not None:
        S = k1.shape[1]
        ins += [k1, k2, v]
        specs += [full(S)] * 3
        n_chunks = S // tk
    return pl.pallas_call(
        functools.partial(_diff_kernel, n_chunks=n_chunks, tk=tk, lam_init=lam_init),
        grid=(B, H_B, Sq // tq),
        in_specs=specs,
        out_specs=tile,
        out_shape=jax.ShapeDtypeStruct((B, Sq, H_B * V_B), BF16),
        scratch_shapes=[_score_scratch(n_chunks, tq, tk)],
        compiler_params=_params(3),
        name="diff",
    )(*ins)


def _merge_kernel(x_ref, mod_ref, oa_ref, ob_ref, sga_ref, sgb_ref, wa_ref, wb_ref, wo_ref, o_ref):
    dot = functools.partial(jnp.dot, preferred_element_type=F32)
    ya = dot(oa_ref[0], wa_ref[...])
    yb = dot(ob_ref[0], wb_ref[...])
    merged = sga_ref[0].astype(F32) * ya + sgb_ref[0].astype(F32) * yb
    o_ref[0] = x_ref[0] + mod_ref[0][2:3] * dot(merged.astype(BF16), wo_ref[...])


def _merge(x, mod_l, mod_row, oa, ob, sga, sgb, w, tm):
    B, S, D = x.shape
    const = lambda shape: pl.BlockSpec(shape, lambda b, i: (0,) * len(shape))
    tok = lambda n: pl.BlockSpec((1, tm, n), lambda b, i: (b, i, 0))
    na, nb = H_A * V_A, H_B * V_B
    return pl.pallas_call(
        _merge_kernel,
        grid=(B, S // tm),
        in_specs=[tok(D), pl.BlockSpec((1, 6, D), lambda b, i: (mod_row(b), 0, 0)),
                  tok(na), tok(nb), tok(D), tok(D),
                  const((na, D)), const((nb, D)), const((D, D))],
        out_specs=tok(D),
        out_shape=jax.ShapeDtypeStruct((B, S, D), F32),
        compiler_params=_params(2),
        name="merge",
    )(x, mod_l, oa, ob, sga, sgb, w["w_br_a"], w["w_br_b"], w["w_out"])


def _ffn_kernel(*refs, tm, final):
    xp_ref, x_ref, xn_ref, mod_ref, g_ref, wa_ref, wb_ref, cwa_ref, cwb_ref, wd_ref = refs[:10]
    gfin_ref = refs[10] if final else None
    o_ref = refs[-1]
    dot = functools.partial(jnp.dot, preferred_element_type=F32)
    i = pl.program_id(1)
    mod = mod_ref[0]
    norm = lambda v: _rms(v, g_ref[...]) * (1.0 + mod[4:5]) + mod[3:4]
    x = x_ref[0]
    keep_prev = (i > 0).astype(F32)
    keep_next = (i < pl.num_programs(1) - 1).astype(F32)
    h = jnp.concatenate([norm(xp_ref[0]) * keep_prev, norm(x), norm(xn_ref[0]) * keep_next],
                        axis=0).astype(BF16)
    rows = tm + 2 * HALO

    def conv(u, cw):
        mid = slice(HALO, HALO + tm)
        return (pltpu.roll(u, 1, 0)[mid] * cw[0:1] + u[mid] * cw[1:2]
                + pltpu.roll(u, rows - 1, 0)[mid] * cw[2:3] + cw[3:4])

    a = conv(dot(h, wa_ref[...]), cwa_ref[...])
    b = conv(dot(h, wb_ref[...]), cwb_ref[...])
    act = (a * jax.nn.sigmoid(a) * b).astype(BF16)
    out = x + mod[5:6] * dot(act, wd_ref[...])
    if final:
        out = _rms(out, gfin_ref[...])
    o_ref[0] = out


def _ffn(x, mod_l, mod_row, g_ffn, w, tm, g_final=None):
    B, S, D = x.shape
    F = w["w_down"].shape[0]
    const = lambda shape: pl.BlockSpec(shape, lambda b, i: (0,) * len(shape))
    r = tm // HALO
    last = S // HALO - 1
    ins = [x, x, x, mod_l, g_ffn, w["w_up_a"], w["w_up_b"], w["conv_a"], w["conv_b"], w["w_down"]]
    specs = [pl.BlockSpec((1, HALO, D), lambda b, i: (b, jnp.maximum(i * r - 1, 0), 0)),
             pl.BlockSpec((1, tm, D), lambda b, i: (b, i, 0)),
             pl.BlockSpec((1, HALO, D), lambda b, i: (b, jnp.minimum((i + 1) * r, last), 0)),
             pl.BlockSpec((1, 6, D), lambda b, i: (mod_row(b), 0, 0)),
             const((1, D)), const((D, F)), const((D, F)),
             const((CONV_W + 1, F)), const((CONV_W + 1, F)), const((F, D))]
    if g_final is not None:
        ins.append(g_final)
        specs.append(const((1, D)))
    return pl.pallas_call(
        functools.partial(_ffn_kernel, tm=tm, final=g_final is not None),
        grid=(B, S // tm),
        in_specs=specs,
        out_specs=pl.BlockSpec((1, tm, D), lambda b, i: (b, i, 0)),
        out_shape=jax.ShapeDtypeStruct((B, S, D), F32),
        compiler_params=_params(2),
        name="ffn",
    )(*ins)


def _slot_pad(w, heads, width, left=0):
    k = w.shape[0]
    w = w.reshape(k, heads, width)
    w = jnp.pad(w, ((0, 0), (0, 0), (left, LANES - width - left)))
    return w.reshape(k, heads * LANES)


def _layer_weights(l, w_in, g_q, w_uq, g_kv, w_ukv, w_br_a, w_br_b, w_out, w_up, conv_w, conv_b, w_down):
    D, F = D_MODEL, D_FF
    cuts = [0, Q_LORA, Q_LORA + KV_LORA, Q_LORA + KV_LORA + ROPE_A]
    wi = w_in[l]
    kr = jnp.pad(wi[:, cuts[2]:cuts[3]], ((0, 0), (NOPE_A, LANES - NOPE_A - ROPE_A)))
    w_in_p = jnp.concatenate([wi[:, :cuts[2]], kr, wi[:, cuts[3]:]], axis=1)
    ukv = w_ukv[l].reshape(KV_LORA, H_A, NOPE_A + V_A)
    kn, va = ukv[..., :NOPE_A], ukv[..., NOPE_A:]
    odd = (jnp.arange(H_A) % 2 == 1)[None, :, None]
    v_slots = jnp.where(odd, jnp.pad(va, ((0, 0), (0, 0), (LANES - V_A, 0))),
                        jnp.pad(va, ((0, 0), (0, 0), (0, LANES - V_A))))
    cw = jnp.concatenate([conv_w[l], conv_b[l][None]], axis=0)
    return {
        "w_in": w_in_p.astype(BF16),
        "g_q": g_q[l][None], "g_kv": g_kv[l][None],
        "w_uq": _slot_pad(w_uq[l], H_A, NOPE_A + ROPE_A).astype(BF16),
        "w_k": _slot_pad(kn.reshape(KV_LORA, H_A * NOPE_A), H_A, NOPE_A).astype(BF16),
        "w_v": v_slots.reshape(KV_LORA, H_A * LANES).astype(BF16),
        "w_br_a": w_br_a[l].astype(BF16), "w_br_b": w_br_b[l].astype(BF16), "w_out": w_out[l].astype(BF16),
        "w_up_a": w_up[l][:, :F].astype(BF16), "w_up_b": w_up[l][:, F:].astype(BF16),
        "conv_a": cw[:, :F], "conv_b": cw[:, F:],
        "w_down": w_down[l].astype(BF16),
    }


def _rope_tables(S, rotary):
    def slot_tables(dr, left, reps):
        q = dr // 4
        if rotary:
            t = jnp.arange(S)
            rows = (t // GRID_W).astype(F32)
            cols = (t % GRID_W).astype(F32)
            freqs = ROPE_BASE ** (-jnp.arange(q, dtype=F32) / q)
            ar, ac = rows[:, None] * freqs, cols[:, None] * freqs
            cos = jnp.concatenate([jnp.cos(ar), jnp.cos(ar), jnp.cos(ac), jnp.cos(ac)], axis=-1)
            sin = jnp.concatenate([-jnp.sin(ar), jnp.sin(ar), -jnp.sin(ac), jnp.sin(ac)], axis=-1)
        else:
            cos, sin = jnp.ones((S, dr), F32), jnp.zeros((S, dr), F32)
        cos, sin = jnp.tile(cos, (1, reps)), jnp.tile(sin, (1, reps))
        pad = ((0, 0), (left, LANES - left - dr * reps))
        return jnp.pad(cos, pad, constant_values=1.0), jnp.pad(sin, pad)

    ca, sa = slot_tables(ROPE_A, NOPE_A, 1)
    cb, sb = slot_tables(DH_B, 0, 2)
    scale_a = (NOPE_A + ROPE_A) ** -0.5 * math.log2(math.e)
    scale_b = DH_B ** -0.5 * math.log2(math.e)
    return jnp.concatenate([ca * scale_a, sa * scale_a, ca, sa, cb * scale_b, sb * scale_b, cb, sb], axis=1)


class _Tiles(NamedTuple):
    proj: int
    merge: int
    ffn: int
    mla_q: int
    mla_k: int
    diff_q: int
    diff_k: int


def _tiles(S):
    return _Tiles(*(min(v, S) for v in (256, 512, 512, 1024, 1024, 512, 1024)))


def kernel(x, c, ctx, c_ctx, w_ada, b_ada, g_mix, g_ffn, w_in, g_q, w_uq, g_kv, w_ukv, lam_q1, lam_k1, lam_q2,
           lam_k2, g_sub, w_br_a, w_br_b, w_out, w_up, conv_w, conv_b, w_down, g_final):
    B, S, D = x.shape
    C = ctx.shape[1]
    depth = w_in.shape[0]
    t = _tiles(S)

    ctx_row = B
    n_rows = -(-(B + 1) // 8) * 8
    cc = jnp.zeros((n_rows, D), F32).at[:B].set(c).at[ctx_row].set(c_ctx)
    mod = _ada(cc, w_ada, b_ada).reshape(depth, n_rows, 6, D)
    lat_row = lambda b: b
    ctx_mod_row = lambda b: ctx_row

    tab_lat = _rope_tables(S, True)
    tab_ctx = _rope_tables(C, False)

    x_lat, x_ctx = x, ctx
    for l in range(depth):
        last = l == depth - 1
        lam_init = 0.8 - 0.6 * math.exp(-0.3 * l)
        w = _layer_weights(l, w_in, g_q, w_uq, g_kv, w_ukv, w_br_a, w_br_b, w_out, w_up, conv_w, conv_b,
                           w_down)
        lams = [v[l][None] for v in (lam_q1, lam_k1, lam_q2, lam_k2)]
        gs = g_sub[l][None]
        gm, gf = g_mix[l][None], g_ffn[l][None]

        pc = _proj(x_ctx, mod[l], ctx_mod_row, gm, w, tab_ctx, min(t.proj, C))
        pq = _proj(x_lat, mod[l], lat_row, gm, w, tab_lat, t.proj)
        qa, ka, va, qb, kb1, kb2, vb, sga, sgb = pq
        cqa, cka, cva, cqb, ckb1, ckb2, cvb, csga, csgb = pc

        o_a = _mla(qa, cka, cva, ka, va, tq=t.mla_q, tk=t.mla_k)
        o_b = _diff(lams, gs, qb, ckb1, ckb2, cvb, kb1, kb2, vb, tq=t.diff_q, tk=t.diff_k, lam_init=lam_init)
        if not last:
            oc_a = _mla(cqa, cka, cva, tq=C, tk=C)
            oc_b = _diff(lams, gs, cqb, ckb1, ckb2, cvb, tq=C, tk=C, lam_init=lam_init)
            x_ctx = _merge(x_ctx, mod[l], ctx_mod_row, oc_a, oc_b, csga, csgb, w, C)
            x_ctx = _ffn(x_ctx, mod[l], ctx_mod_row, gf, w, C)
        x_lat = _merge(x_lat, mod[l], lat_row, o_a, o_b, sga, sgb, w, t.merge)
        x_lat = _ffn(x_lat, mod[l], lat_row, gf, w, t.ffn, g_final[None] if last else None)
    return x_lat
```

```python
import functools
import math
from typing import NamedTuple

import jax
import jax.numpy as jnp
from jax import lax
from jax.experimental import pallas as pl
from jax.experimental.pallas import tpu as pltpu

D_MODEL = 1024
GRID_W = 64
EPS = 1e-6
ROPE_BASE = 10000.0

H_A = 8
Q_LORA = 384
KV_LORA = 256
NOPE_A = 64
ROPE_A = 32
V_A = 64

H_B = 4
DH_B = 64
V_B = 2 * DH_B

D_FF = 2816
CONV_W = 3
QKV_B = H_B * 2 * DH_B

LANES = 128
HALO = 8
VMEM_LIMIT = 56 * 1024 * 1024

C_Q = 0
C_KV = C_Q + Q_LORA
C_KR = C_KV + KV_LORA
C_DQ = C_KR + LANES
C_DK = C_DQ + QKV_B
C_DV = C_DK + QKV_B
C_GA = C_DV + QKV_B
C_GB = C_GA + D_MODEL
IN_WP = C_GB + D_MODEL

BF16 = jnp.bfloat16
F32 = jnp.float32


def _params(n_grid):
    return pltpu.CompilerParams(dimension_semantics=("parallel",) * n_grid,
                                vmem_limit_bytes=VMEM_LIMIT)


def _rms(v, g):
    ms = jnp.mean(v * v, axis=-1, keepdims=True)
    return v * lax.rsqrt(ms + EPS) * g


def _swap_groups(v, q):
    n = v.shape[-1]
    lane = lax.broadcasted_iota(jnp.int32, v.shape, 1)
    ahead = pltpu.roll(v, n - q, 1)
    behind = pltpu.roll(v, q, 1)
    return jnp.where((lane & (2 * q - 1)) < q, ahead, behind)


def _rope(v, cos, sin, q):
    return v * cos + _swap_groups(v, q) * sin


def _ada_kernel(c_ref, w_ref, b_ref, o_ref):
    a = c_ref[...]
    a = a * jax.nn.sigmoid(a)
    w = w_ref[0]
    a_hi = a.astype(BF16)
    a_lo = (a - a_hi.astype(F32)).astype(BF16)
    w_hi = w.astype(BF16)
    w_lo = (w - w_hi.astype(F32)).astype(BF16)
    dot = functools.partial(jnp.dot, preferred_element_type=F32)
    o_ref[0] = dot(a_hi, w_hi) + dot(a_lo, w_hi) + dot(a_hi, w_lo) + b_ref[0]


def _ada(cc, w_ada, b_ada):
    L, D, N = w_ada.shape
    R = cc.shape[0]
    tn = 1024
    return pl.pallas_call(
        _ada_kernel,
        grid=(L, N // tn),
        in_specs=[pl.BlockSpec((R, D), lambda l, j: (0, 0)),
                  pl.BlockSpec((1, D, tn), lambda l, j: (l, 0, j)),
                  pl.BlockSpec((1, 1, tn), lambda l, j: (l, 0, j))],
        out_specs=pl.BlockSpec((1, R, tn), lambda l, j: (l, 0, j)),
        out_shape=jax.ShapeDtypeStruct((L, R, N), F32),
        compiler_params=_params(2),
        name="adaln",
    )(cc, w_ada, b_ada.reshape(L, 1, N))


def _proj_kernel(x_ref, mod_ref, gmix_ref, win_ref, gq_ref, wuq_ref, gkv_ref, wk_ref, wv_ref, tab_ref,
                 qa_ref, ka_ref, va_ref, qb_ref, kb1_ref, kb2_ref, vb_ref, sga_ref, sgb_ref):
    dot = functools.partial(jnp.dot, preferred_element_type=F32)
    x = x_ref[0]
    mod = mod_ref[0]
    h = (_rms(x, gmix_ref[...]) * (1.0 + mod[1:2]) + mod[0:1]).astype(BF16)
    tm = x.shape[0]
    lane = lax.broadcasted_iota(jnp.int32, (tm, LANES), 1)
    low = lane < (LANES // 2)
    tab = lambda k: tab_ref[:, k * LANES:(k + 1) * LANES]
    slot = lambda k: slice(k * LANES, (k + 1) * LANES)

    z = dot(h, win_ref[...])
    cq = z[:, C_Q:C_Q + Q_LORA]
    q = dot(_rms(cq, gq_ref[...]).astype(BF16), wuq_ref[...])
    for hd in range(H_A):
        qa_ref[0, :, slot(hd)] = _rope(q[:, slot(hd)], tab(0), tab(1), ROPE_A // 4).astype(BF16)
    ckv = _rms(z[:, C_KV:C_KV + KV_LORA], gkv_ref[...]).astype(BF16)
    kn = dot(ckv, wk_ref[...])
    vv = dot(ckv, wv_ref[...])
    kr = _rope(z[:, C_KR:C_KR + LANES], tab(2), tab(3), ROPE_A // 4)
    for hd in range(H_A):
        ka_ref[0, :, slot(hd)] = (kn[:, slot(hd)] + kr).astype(BF16)
        ones = jnp.where(low, 0.0, 1.0) if hd % 2 == 0 else jnp.where(low, 1.0, 0.0)
        va_ref[0, :, slot(hd)] = (vv[:, slot(hd)] + ones).astype(BF16)

    dq = z[:, C_DQ:C_DQ + QKV_B]
    dk = z[:, C_DK:C_DK + QKV_B]
    for hd in range(H_B):
        qb_ref[0, :, slot(hd)] = _rope(dq[:, slot(hd)], tab(4), tab(5), DH_B // 4).astype(BF16)
        k = _rope(dk[:, slot(hd)], tab(6), tab(7), DH_B // 4)
        kb1_ref[0, :, slot(hd)] = jnp.where(low, k, 0.0).astype(BF16)
        kb2_ref[0, :, slot(hd)] = jnp.where(low, 0.0, k).astype(BF16)
    vb_ref[0] = z[:, C_DV:C_DV + QKV_B].astype(BF16)

    sga_ref[0] = jax.nn.sigmoid(z[:, C_GA:C_GA + D_MODEL]).astype(sga_ref.dtype)
    sgb_ref[0] = jax.nn.sigmoid(z[:, C_GB:C_GB + D_MODEL]).astype(sgb_ref.dtype)


def _proj(x, mod_l, mod_row, g_mix, w, tab, tm):
    B, S, D = x.shape
    const = lambda shape: pl.BlockSpec(shape, lambda i, b: (0,) * len(shape))
    tok = lambda n: pl.BlockSpec((1, tm, n), lambda i, b: (b, i, 0))
    wa, wb = H_A * LANES, H_B * LANES
    outs = [(wa, BF16), (wa, BF16), (wa, BF16), (wb, BF16), (wb, BF16), (wb, BF16), (wb, BF16),
            (D, BF16), (D, BF16)]
    return pl.pallas_call(
        _proj_kernel,
        grid=(S // tm, B),
        in_specs=[tok(D),
                  pl.BlockSpec((1, 6, D), lambda i, b: (mod_row(b), 0, 0)),
                  const((1, D)), const((D, IN_WP)), const((1, Q_LORA)), const((Q_LORA, wa)),
                  const((1, KV_LORA)), const((KV_LORA, wa)), const((KV_LORA, wa)),
                  pl.BlockSpec((tm, 8 * LANES), lambda i, b: (i, 0))],
        out_specs=[tok(n) for n, _ in outs],
        out_shape=[jax.ShapeDtypeStruct((B, S, n), dt) for n, dt in outs],
        compiler_params=_params(2),
        name="proj",
    )(x, mod_l, g_mix, w["w_in"], w["g_q"], w["w_uq"], w["g_kv"], w["w_k"], w["w_v"], tab)


def _qk(q, k):
    return lax.dot_general(q, k, (((1,), (1,)), ((), ())), preferred_element_type=F32)


def _absorb(s, v, state, exp_dtype):
    smax = jnp.max(s, axis=-1, keepdims=True)
    m_new = smax if state is None else jnp.maximum(state[0], smax)
    pv = jnp.dot(jnp.exp2((s - m_new).astype(exp_dtype)).astype(BF16), v, preferred_element_type=F32)
    if state is None:
        return m_new, pv
    return m_new, jnp.exp2(state[0] - m_new) * state[1] + pv


def _flash(n_streams, n_chunks, tk, score_ctx, score, value_ctx, value, s_scr, exp_dtype):
    streams = range(n_streams)
    states = tuple(_absorb(score_ctx(t), value_ctx(t), None, exp_dtype) for t in streams)
    if not n_chunks:
        return states
    rows = lambda i: pl.ds(i * tk, tk)
    for t in streams:
        s_scr[0, t] = score(t, rows(0))
    for i in range(n_chunks):
        slot, out = i % 2, []
        for t in streams:
            if i + 1 < n_chunks:
                s_scr[1 - slot, t] = score(t, rows(i + 1))
            out.append(_absorb(s_scr[slot, t], value(t, rows(i)), states[t], exp_dtype))
        states = tuple(out)
    return states


def _score_scratch(n_chunks, tq, tk):
    return pltpu.VMEM((2, 2, tq, tk) if n_chunks else (1, 1, 8, LANES), F32)


def _mla_kernel(*refs, n_chunks, tk):
    if n_chunks:
        q_ref, kc_ref, vc_ref, k_ref, v_ref, o_ref, s_scr = refs
    else:
        (q_ref, kc_ref, vc_ref, o_ref, s_scr), k_ref, v_ref = refs, None, None
    sl = lambda t: slice(t * LANES, (t + 1) * LANES)
    states = _flash(
        2, n_chunks, tk,
        score_ctx=lambda t: _qk(q_ref[0, :, sl(t)], kc_ref[0, :, sl(t)]),
        score=lambda t, rows: _qk(q_ref[0, :, sl(t)], k_ref[0, rows, sl(t)]),
        value_ctx=lambda t: vc_ref[0, :, sl(t)],
        value=lambda t, rows: v_ref[0, rows, sl(t)],
        s_scr=s_scr, exp_dtype=BF16)
    acc0, acc1 = states[0][1], states[1][1]
    lane = lax.broadcasted_iota(jnp.int32, acc0.shape, 1)
    o = jnp.where(lane < LANES // 2,
                  acc0 / pltpu.roll(acc0, LANES // 2, 1),
                  acc1 / pltpu.roll(acc1, LANES // 2, 1))
    o_ref[0] = o.astype(o_ref.dtype)


def _mla(q, kc, vc, k=None, v=None, *, tq, tk):
    B, Sq, _ = q.shape
    C = kc.shape[1]
    w = 2 * LANES
    qspec = pl.BlockSpec((1, tq, w), lambda b, h, i: (b, i, h))
    full = lambda n: pl.BlockSpec((1, n, w), lambda b, h, i: (b, 0, h))
    ins, specs, n_chunks = [q, kc, vc], [qspec, full(C), full(C)], 0
    if k is not None:
        S = k.shape[1]
        ins += [k, v]
        specs += [full(S), full(S)]
        n_chunks = S // tk
    return pl.pallas_call(
        functools.partial(_mla_kernel, n_chunks=n_chunks, tk=tk),
        grid=(B, H_A // 2, Sq // tq),
        in_specs=specs,
        out_specs=pl.BlockSpec((1, tq, LANES), lambda b, h, i: (b, i, h)),
        out_shape=jax.ShapeDtypeStruct((B, Sq, H_A * V_A), BF16),
        scratch_shapes=[_score_scratch(n_chunks, tq, tk)],
        compiler_params=_params(3),
        name="mla",
    )(*ins)


def _diff_kernel(*refs, n_chunks, tk, lam_init):
    lq1_ref, lk1_ref, lq2_ref, lk2_ref, gsub_ref, q_ref, k1c_ref, k2c_ref, vc_ref = refs[:9]
    if n_chunks:
        k1_ref, k2_ref, v_ref, o_ref, s_scr = refs[9:]
    else:
        (o_ref, s_scr), k1_ref, k2_ref, v_ref = refs[9:], None, None, None
    lam = (jnp.exp(jnp.sum(lq1_ref[...] * lk1_ref[...], axis=-1, keepdims=True))
           - jnp.exp(jnp.sum(lq2_ref[...] * lk2_ref[...], axis=-1, keepdims=True)) + lam_init)
    kc_refs, k_refs = (k1c_ref, k2c_ref), (k1_ref, k2_ref)
    ext = lambda v: jnp.concatenate([v, jnp.ones_like(v)], axis=1)
    (_, a1), (_, a2) = _flash(
        2, n_chunks, tk,
        score_ctx=lambda t: _qk(q_ref[0], kc_refs[t][0]),
        score=lambda t, rows: _qk(q_ref[0], k_refs[t][0, rows, :]),
        value_ctx=lambda t: ext(vc_ref[0]),
        value=lambda t, rows: ext(v_ref[0, rows, :]),
        s_scr=s_scr, exp_dtype=F32)
    o = a1[:, :LANES] / a1[:, LANES:] - lam * (a2[:, :LANES] / a2[:, LANES:])
    o_ref[0] = (_rms(o, gsub_ref[...]) * (1.0 - lam_init)).astype(o_ref.dtype)


def _diff(lams, g_sub, q, k1c, k2c, vc, k1=None, k2=None, v=None, *, tq, tk, lam_init):
    B, Sq, _ = q.shape
    C = k1c.shape[1]
    small = lambda n: pl.BlockSpec((1, n), lambda b, h, i: (0, 0))
    full = lambda n: pl.BlockSpec((1, n, LANES), lambda b, h, i: (b, 0, h))
    tile = pl.BlockSpec((1, tq, LANES), lambda b, h, i: (b, i, h))
    ins = list(lams) + [g_sub, q, k1c, k2c, vc]
    specs = [small(DH_B)] * 4 + [small(V_B), tile, full(C), full(C), full(C)]
    n_chunks = 0
    if k1 is not None:
        S = k1.shape[1]
        ins += [k1, k2, v]
        specs += [full(S)] * 3
        n_chunks = S // tk
    return pl.pallas_call(
        functools.partial(_diff_kernel, n_chunks=n_chunks, tk=tk, lam_init=lam_init),
        grid=(B, H_B, Sq // tq),
        in_specs=specs,
        out_specs=tile,
        out_shape=jax.ShapeDtypeStruct((B, Sq, H_B * V_B), BF16),
        scratch_shapes=[_score_scratch(n_chunks, tq, tk)],
        compiler_params=_params(3),
        name="diff",
    )(*ins)


def _merge_kernel(x_ref, mod_ref, oa_ref, ob_ref, sga_ref, sgb_ref, wa_ref, wb_ref, wo_ref, o_ref):
    dot = functools.partial(jnp.dot, preferred_element_type=F32)
    ya = dot(oa_ref[0], wa_ref[...])
    yb = dot(ob_ref[0], wb_ref[...])
    merged = sga_ref[0].astype(F32) * ya + sgb_ref[0].astype(F32) * yb
    o_ref[0] = x_ref[0] + mod_ref[0][2:3] * dot(merged.astype(BF16), wo_ref[...])


def _merge(x, mod_l, mod_row, oa, ob, sga, sgb, w, tm):
    B, S, D = x.shape
    const = lambda shape: pl.BlockSpec(shape, lambda b, i: (0,) * len(shape))
    tok = lambda n: pl.BlockSpec((1, tm, n), lambda b, i: (b, i, 0))
    na, nb = H_A * V_A, H_B * V_B
    return pl.pallas_call(
        _merge_kernel,
        grid=(B, S // tm),
        in_specs=[tok(D), pl.BlockSpec((1, 6, D), lambda b, i: (mod_row(b), 0, 0)),
                  tok(na), tok(nb), tok(D), tok(D),
                  const((na, D)), const((nb, D)), const((D, D))],
        out_specs=tok(D),
        out_shape=jax.ShapeDtypeStruct((B, S, D), F32),
        compiler_params=_params(2),
        name="merge",
    )(x, mod_l, oa, ob, sga, sgb, w["w_br_a"], w["w_br_b"], w["w_out"])


def _ffn_kernel(*refs, tm, final):
    xp_ref, x_ref, xn_ref, mod_ref, g_ref, wa_ref, wb_ref, cwa_ref, cwb_ref, wd_ref = refs[:10]
    gfin_ref = refs[10] if final else None
    o_ref = refs[-1]
    dot = functools.partial(jnp.dot, preferred_element_type=F32)
    i = pl.program_id(1)
    mod = mod_ref[0]
    norm = lambda v: _rms(v, g_ref[...]) * (1.0 + mod[4:5]) + mod[3:4]
    x = x_ref[0]
    keep_prev = (i > 0).astype(F32)
    keep_next = (i < pl.num_programs(1) - 1).astype(F32)
    h = jnp.concatenate([norm(xp_ref[0]) * keep_prev, norm(x), norm(xn_ref[0]) * keep_next],
                        axis=0).astype(BF16)
    rows = tm + 2 * HALO

    def conv(u, cw):
        mid = slice(HALO, HALO + tm)
        return (pltpu.roll(u, 1, 0)[mid] * cw[0:1] + u[mid] * cw[1:2]
                + pltpu.roll(u, rows - 1, 0)[mid] * cw[2:3] + cw[3:4])

    a = conv(dot(h, wa_ref[...]), cwa_ref[...])
    b = conv(dot(h, wb_ref[...]), cwb_ref[...])
    act = (a * jax.nn.sigmoid(a) * b).astype(BF16)
    out = x + mod[5:6] * dot(act, wd_ref[...])
    if final:
        out = _rms(out, gfin_ref[...])
    o_ref[0] = out


def _ffn(x, mod_l, mod_row, g_ffn, w, tm, g_final=None):
    B, S, D = x.shape
    F = w["w_down"].shape[0]
    const = lambda shape: pl.BlockSpec(shape, lambda b, i: (0,) * len(shape))
    r = tm // HALO
    last = S // HALO - 1
    ins = [x, x, x, mod_l, g_ffn, w["w_up_a"], w["w_up_b"], w["conv_a"], w["conv_b"], w["w_down"]]
    specs = [pl.BlockSpec((1, HALO, D), lambda b, i: (b, jnp.maximum(i * r - 1, 0), 0)),
             pl.BlockSpec((1, tm, D), lambda b, i: (b, i, 0)),
             pl.BlockSpec((1, HALO, D), lambda b, i: (b, jnp.minimum((i + 1) * r, last), 0)),
             pl.BlockSpec((1, 6, D), lambda b, i: (mod_row(b), 0, 0)),
             const((1, D)), const((D, F)), const((D, F)),
             const((CONV_W + 1, F)), const((CONV_W + 1, F)), const((F, D))]
    if g_final is not None:
        ins.append(g_final)
        specs.append(const((1, D)))
    return pl.pallas_call(
        functools.partial(_ffn_kernel, tm=tm, final=g_final is not None),
        grid=(B, S // tm),
        in_specs=specs,
        out_specs=pl.BlockSpec((1, tm, D), lambda b, i: (b, i, 0)),
        out_shape=jax.ShapeDtypeStruct((B, S, D), F32),
        compiler_params=_params(2),
        name="ffn",
    )(*ins)


def _slot_pad(w, heads, width, left=0):
    k = w.shape[0]
    w = w.reshape(k, heads, width)
    w = jnp.pad(w, ((0, 0), (0, 0), (left, LANES - width - left)))
    return w.reshape(k, heads * LANES)


def _layer_weights(l, w_in, g_q, w_uq, g_kv, w_ukv, w_br_a, w_br_b, w_out, w_up, conv_w, conv_b, w_down):
    D, F = D_MODEL, D_FF
    cuts = [0, Q_LORA, Q_LORA + KV_LORA, Q_LORA + KV_LORA + ROPE_A]
    wi = w_in[l]
    kr = jnp.pad(wi[:, cuts[2]:cuts[3]], ((0, 0), (NOPE_A, LANES - NOPE_A - ROPE_A)))
    w_in_p = jnp.concatenate([wi[:, :cuts[2]], kr, wi[:, cuts[3]:]], axis=1)
    ukv = w_ukv[l].reshape(KV_LORA, H_A, NOPE_A + V_A)
    kn, va = ukv[..., :NOPE_A], ukv[..., NOPE_A:]
    odd = (jnp.arange(H_A) % 2 == 1)[None, :, None]
    v_slots = jnp.where(odd, jnp.pad(va, ((0, 0), (0, 0), (LANES - V_A, 0))),
                        jnp.pad(va, ((0, 0), (0, 0), (0, LANES - V_A))))
    cw = jnp.concatenate([conv_w[l], conv_b[l][None]], axis=0)
    return {
        "w_in": w_in_p.astype(BF16),
        "g_q": g_q[l][None], "g_kv": g_kv[l][None],
        "w_uq": _slot_pad(w_uq[l], H_A, NOPE_A + ROPE_A).astype(BF16),
        "w_k": _slot_pad(kn.reshape(KV_LORA, H_A * NOPE_A), H_A, NOPE_A).astype(BF16),
        "w_v": v_slots.reshape(KV_LORA, H_A * LANES).astype(BF16),
        "w_br_a": w_br_a[l].astype(BF16), "w_br_b": w_br_b[l].astype(BF16), "w_out": w_out[l].astype(BF16),
        "w_up_a": w_up[l][:, :F].astype(BF16), "w_up_b": w_up[l][:, F:].astype(BF16),
        "conv_a": cw[:, :F], "conv_b": cw[:, F:],
        "w_down": w_down[l].astype(BF16),
    }


def _rope_tables(S, rotary):
    def slot_tables(dr, left, reps):
        q = dr // 4
        if rotary:
            t = jnp.arange(S)
            rows = (t // GRID_W).astype(F32)
            cols = (t % GRID_W).astype(F32)
            freqs = ROPE_BASE ** (-jnp.arange(q, dtype=F32) / q)
            ar, ac = rows[:, None] * freqs, cols[:, None] * freqs
            cos = jnp.concatenate([jnp.cos(ar), jnp.cos(ar), jnp.cos(ac), jnp.cos(ac)], axis=-1)
            sin = jnp.concatenate([-jnp.sin(ar), jnp.sin(ar), -jnp.sin(ac), jnp.sin(ac)], axis=-1)
        else:
            cos, sin = jnp.ones((S, dr), F32), jnp.zeros((S, dr), F32)
        cos, sin = jnp.tile(cos, (1, reps)), jnp.tile(sin, (1, reps))
        pad = ((0, 0), (left, LANES - left - dr * reps))
        return jnp.pad(cos, pad, constant_values=1.0), jnp.pad(sin, pad)

    ca, sa = slot_tables(ROPE_A, NOPE_A, 1)
    cb, sb = slot_tables(DH_B, 0, 2)
    scale_a = (NOPE_A + ROPE_A) ** -0.5 * math.log2(math.e)
    scale_b = DH_B ** -0.5 * math.log2(math.e)
    return jnp.concatenate([ca * scale_a, sa * scale_a, ca, sa, cb * scale_b, sb * scale_b, cb, sb], axis=1)


class _Tiles(NamedTuple):
    proj: int
    merge: int
    ffn: int
    mla_q: int
    mla_k: int
    diff_q: int
    diff_k: int


def _tiles(S):
    return _Tiles(*(min(v, S) for v in (256, 512, 512, 1024, 1024, 1024, 512)))


def kernel(x, c, ctx, c_ctx, w_ada, b_ada, g_mix, g_ffn, w_in, g_q, w_uq, g_kv, w_ukv, lam_q1, lam_k1, lam_q2,
           lam_k2, g_sub, w_br_a, w_br_b, w_out, w_up, conv_w, conv_b, w_down, g_final):
    B, S, D = x.shape
    C = ctx.shape[1]
    depth = w_in.shape[0]
    t = _tiles(S)

    ctx_row = B
    n_rows = -(-(B + 1) // 8) * 8
    cc = jnp.zeros((n_rows, D), F32).at[:B].set(c).at[ctx_row].set(c_ctx)
    mod = _ada(cc, w_ada, b_ada).reshape(depth, n_rows, 6, D)
    lat_row = lambda b: b
    ctx_mod_row = lambda b: ctx_row

    tab_lat = _rope_tables(S, True)
    tab_ctx = _rope_tables(C, False)

    x_lat, x_ctx = x, ctx
    for l in range(depth):
        last = l == depth - 1
        lam_init = 0.8 - 0.6 * math.exp(-0.3 * l)
        w = _layer_weights(l, w_in, g_q, w_uq, g_kv, w_ukv, w_br_a, w_br_b, w_out, w_up, conv_w, conv_b,
                           w_down)
        lams = [v[l][None] for v in (lam_q1, lam_k1, lam_q2, lam_k2)]
        gs = g_sub[l][None]
        gm, gf = g_mix[l][None], g_ffn[l][None]

        pc = _proj(x_ctx, mod[l], ctx_mod_row, gm, w, tab_ctx, min(t.proj, C))
        pq = _proj(x_lat, mod[l], lat_row, gm, w, tab_lat, t.proj)
        qa, ka, va, qb, kb1, kb2, vb, sga, sgb = pq
        cqa, cka, cva, cqb, ckb1, ckb2, cvb, csga, csgb = pc

        o_a = _mla(qa, cka, cva, ka, va, tq=t.mla_q, tk=t.mla_k)
        o_b = _diff(lams, gs, qb, ckb1, ckb2, cvb, kb1, kb2, vb, tq=t.diff_q, tk=t.diff_k, lam_init=lam_init)
        if not last:
            oc_a = _mla(cqa, cka, cva, tq=C, tk=C)
            oc_b = _diff(lams, gs, cqb, ckb1, ckb2, cvb, tq=C, tk=C, lam_init=lam_init)
            x_ctx = _merge(x_ctx, mod[l], ctx_mod_row, oc_a, oc_b, csga, csgb, w, C)
            x_ctx = _ffn(x_ctx, mod[l], ctx_mod_row, gf, w, C)
        x_lat = _merge(x_lat, mod[l], lat_row, o_a, o_b, sga, sgb, w, t.merge)
        x_lat = _ffn(x_lat, mod[l], lat_row, gf, w, t.ffn, g_final[None] if last else None)
    return x_lat
```

```python
import functools
import math
from typing import NamedTuple

import jax
import jax.numpy as jnp
from jax import lax
from jax.experimental import pallas as pl
from jax.experimental.pallas import tpu as pltpu

D_MODEL = 1024
GRID_W = 64
EPS = 1e-6
ROPE_BASE = 10000.0

H_A = 8
Q_LORA = 384
KV_LORA = 256
NOPE_A = 64
ROPE_A = 32
V_A = 64

H_B = 4
DH_B = 64
V_B = 2 * DH_B

D_FF = 2816
CONV_W = 3
QKV_B = H_B * 2 * DH_B

LANES = 128
HALO = 8
VMEM_LIMIT = 56 * 1024 * 1024

C_Q = 0
C_KV = C_Q + Q_LORA
C_KR = C_KV + KV_LORA
C_DQ = C_KR + LANES
C_DK = C_DQ + QKV_B
C_DV = C_DK + QKV_B
C_GA = C_DV + QKV_B
C_GB = C_GA + D_MODEL
IN_WP = C_GB + D_MODEL

BF16 = jnp.bfloat16
F32 = jnp.float32


def _params(n_grid):
    return pltpu.CompilerParams(dimension_semantics=("parallel",) * n_grid,
                                vmem_limit_bytes=VMEM_LIMIT)


def _rms(v, g):
    ms = jnp.mean(v * v, axis=-1, keepdims=True)
    return v * lax.rsqrt(ms + EPS) * g


def _swap_groups(v, q):
    n = v.shape[-1]
    lane = lax.broadcasted_iota(jnp.int32, v.shape, 1)
    ahead = pltpu.roll(v, n - q, 1)
    behind = pltpu.roll(v, q, 1)
    return jnp.where((lane & (2 * q - 1)) < q, ahead, behind)


def _rope(v, cos, sin, q):
    return v * cos + _swap_groups(v, q) * sin


def _ada_kernel(c_ref, w_ref, b_ref, o_ref):
    a = c_ref[...]
    a = a * jax.nn.sigmoid(a)
    w = w_ref[0]
    a_hi = a.astype(BF16)
    a_lo = (a - a_hi.astype(F32)).astype(BF16)
    w_hi = w.astype(BF16)
    w_lo = (w - w_hi.astype(F32)).astype(BF16)
    dot = functools.partial(jnp.dot, preferred_element_type=F32)
    o_ref[0] = dot(a_hi, w_hi) + dot(a_lo, w_hi) + dot(a_hi, w_lo) + b_ref[0]


def _ada(cc, w_ada, b_ada):
    L, D, N = w_ada.shape
    R = cc.shape[0]
    tn = 1024
    return pl.pallas_call(
        _ada_kernel,
        grid=(L, N // tn),
        in_specs=[pl.BlockSpec((R, D), lambda l, j: (0, 0)),
                  pl.BlockSpec((1, D, tn), lambda l, j: (l, 0, j)),
                  pl.BlockSpec((1, 1, tn), lambda l, j: (l, 0, j))],
        out_specs=pl.BlockSpec((1, R, tn), lambda l, j: (l, 0, j)),
        out_shape=jax.ShapeDtypeStruct((L, R, N), F32),
        compiler_params=_params(2),
        name="adaln",
    )(cc, w_ada, b_ada.reshape(L, 1, N))


def _proj_kernel(x_ref, mod_ref, gmix_ref, win_ref, gq_ref, wuq_ref, gkv_ref, wk_ref, wv_ref, tab_ref,
                 qa_ref, ka_ref, va_ref, qb_ref, kb1_ref, kb2_ref, vb_ref, sga_ref, sgb_ref):
    dot = functools.partial(jnp.dot, preferred_element_type=F32)
    x = x_ref[0]
    mod = mod_ref[0]
    h = (_rms(x, gmix_ref[...]) * (1.0 + mod[1:2]) + mod[0:1]).astype(BF16)
    tm = x.shape[0]
    lane = lax.broadcasted_iota(jnp.int32, (tm, LANES), 1)
    low = lane < (LANES // 2)
    tab = lambda k: tab_ref[:, k * LANES:(k + 1) * LANES]
    slot = lambda k: slice(k * LANES, (k + 1) * LANES)

    z = dot(h, win_ref[...])
    cq = z[:, C_Q:C_Q + Q_LORA]
    q = dot(_rms(cq, gq_ref[...]).astype(BF16), wuq_ref[...])
    for hd in range(H_A):
        qa_ref[0, :, slot(hd)] = _rope(q[:, slot(hd)], tab(0), tab(1), ROPE_A // 4).astype(BF16)
    ckv = _rms(z[:, C_KV:C_KV + KV_LORA], gkv_ref[...]).astype(BF16)
    kn = dot(ckv, wk_ref[...])
    vv = dot(ckv, wv_ref[...])
    kr = _rope(z[:, C_KR:C_KR + LANES], tab(2), tab(3), ROPE_A // 4)
    for hd in range(H_A):
        ka_ref[0, :, slot(hd)] = (kn[:, slot(hd)] + kr).astype(BF16)
        ones = jnp.where(low, 0.0, 1.0) if hd % 2 == 0 else jnp.where(low, 1.0, 0.0)
        va_ref[0, :, slot(hd)] = (vv[:, slot(hd)] + ones).astype(BF16)

    dq = z[:, C_DQ:C_DQ + QKV_B]
    dk = z[:, C_DK:C_DK + QKV_B]
    for hd in range(H_B):
        qb_ref[0, :, slot(hd)] = _rope(dq[:, slot(hd)], tab(4), tab(5), DH_B // 4).astype(BF16)
        k = _rope(dk[:, slot(hd)], tab(6), tab(7), DH_B // 4)
        kb1_ref[0, :, slot(hd)] = jnp.where(low, k, 0.0).astype(BF16)
        kb2_ref[0, :, slot(hd)] = jnp.where(low, 0.0, k).astype(BF16)
    vb_ref[0] = z[:, C_DV:C_DV + QKV_B].astype(BF16)

    sga_ref[0] = jax.nn.sigmoid(z[:, C_GA:C_GA + D_MODEL]).astype(sga_ref.dtype)
    sgb_ref[0] = jax.nn.sigmoid(z[:, C_GB:C_GB + D_MODEL]).astype(sgb_ref.dtype)


def _proj(x, mod_l, mod_row, g_mix, w, tab, tm):
    B, S, D = x.shape
    const = lambda shape: pl.BlockSpec(shape, lambda i, b: (0,) * len(shape))
    tok = lambda n: pl.BlockSpec((1, tm, n), lambda i, b: (b, i, 0))
    wa, wb = H_A * LANES, H_B * LANES
    outs = [(wa, BF16), (wa, BF16), (wa, BF16), (wb, BF16), (wb, BF16), (wb, BF16), (wb, BF16),
            (D, BF16), (D, BF16)]
    return pl.pallas_call(
        _proj_kernel,
        grid=(S // tm, B),
        in_specs=[tok(D),
                  pl.BlockSpec((1, 6, D), lambda i, b: (mod_row(b), 0, 0)),
                  const((1, D)), const((D, IN_WP)), const((1, Q_LORA)), const((Q_LORA, wa)),
                  const((1, KV_LORA)), const((KV_LORA, wa)), const((KV_LORA, wa)),
                  pl.BlockSpec((tm, 8 * LANES), lambda i, b: (i, 0))],
        out_specs=[tok(n) for n, _ in outs],
        out_shape=[jax.ShapeDtypeStruct((B, S, n), dt) for n, dt in outs],
        compiler_params=_params(2),
        name="proj",
    )(x, mod_l, g_mix, w["w_in"], w["g_q"], w["w_uq"], w["g_kv"], w["w_k"], w["w_v"], tab)


def _qk(q, k):
    return lax.dot_general(q, k, (((1,), (1,)), ((), ())), preferred_element_type=F32)


def _absorb(s, v, state, exp_dtype):
    smax = jnp.max(s, axis=-1, keepdims=True)
    m_new = smax if state is None else jnp.maximum(state[0], smax)
    pv = jnp.dot(jnp.exp2((s - m_new).astype(exp_dtype)).astype(BF16), v, preferred_element_type=F32)
    if state is None:
        return m_new, pv
    return m_new, jnp.exp2(state[0] - m_new) * state[1] + pv


def _flash(n_streams, n_chunks, tk, score_ctx, score, value_ctx, value, s_scr, exp_dtype):
    streams = range(n_streams)
    states = tuple(_absorb(score_ctx(t), value_ctx(t), None, exp_dtype) for t in streams)
    if not n_chunks:
        return states
    rows = lambda i: pl.ds(i * tk, tk)
    for t in streams:
        s_scr[0, t] = score(t, rows(0))
    for i in range(n_chunks):
        slot, out = i % 2, []
        for t in streams:
            if i + 1 < n_chunks:
                s_scr[1 - slot, t] = score(t, rows(i + 1))
            out.append(_absorb(s_scr[slot, t], value(t, rows(i)), states[t], exp_dtype))
        states = tuple(out)
    return states


def _score_scratch(n_chunks, tq, tk):
    return pltpu.VMEM((2, 2, tq, tk) if n_chunks else (1, 1, 8, LANES), F32)


def _mla_kernel(*refs, n_chunks, tk):
    if n_chunks:
        q_ref, kc_ref, vc_ref, k_ref, v_ref, o_ref, s_scr = refs
    else:
        (q_ref, kc_ref, vc_ref, o_ref, s_scr), k_ref, v_ref = refs, None, None
    sl = lambda t: slice(t * LANES, (t + 1) * LANES)
    states = _flash(
        2, n_chunks, tk,
        score_ctx=lambda t: _qk(q_ref[0, :, sl(t)], kc_ref[0, :, sl(t)]),
        score=lambda t, rows: _qk(q_ref[0, :, sl(t)], k_ref[0, rows, sl(t)]),
        value_ctx=lambda t: vc_ref[0, :, sl(t)],
        value=lambda t, rows: v_ref[0, rows, sl(t)],
        s_scr=s_scr, exp_dtype=BF16)
    acc0, acc1 = states[0][1], states[1][1]
    lane = lax.broadcasted_iota(jnp.int32, acc0.shape, 1)
    o = jnp.where(lane < LANES // 2,
                  acc0 / pltpu.roll(acc0, LANES // 2, 1),
                  acc1 / pltpu.roll(acc1, LANES // 2, 1))
    o_ref[0] = o.astype(o_ref.dtype)


def _mla(q, kc, vc, k=None, v=None, *, tq, tk):
    B, Sq, _ = q.shape
    C = kc.shape[1]
    w = 2 * LANES
    qspec = pl.BlockSpec((1, tq, w), lambda b, h, i: (b, i, h))
    full = lambda n: pl.BlockSpec((1, n, w), lambda b, h, i: (b, 0, h))
    ins, specs, n_chunks = [q, kc, vc], [qspec, full(C), full(C)], 0
    if k is not None:
        S = k.shape[1]
        ins += [k, v]
        specs += [full(S), full(S)]
        n_chunks = S // tk
    return pl.pallas_call(
        functools.partial(_mla_kernel, n_chunks=n_chunks, tk=tk),
        grid=(B, H_A // 2, Sq // tq),
        in_specs=specs,
        out_specs=pl.BlockSpec((1, tq, LANES), lambda b, h, i: (b, i, h)),
        out_shape=jax.ShapeDtypeStruct((B, Sq, H_A * V_A), BF16),
        scratch_shapes=[_score_scratch(n_chunks, tq, tk)],
        compiler_params=_params(3),
        name="mla",
    )(*ins)


def _diff_kernel(*refs, n_chunks, tk, lam_init):
    lq1_ref, lk1_ref, lq2_ref, lk2_ref, gsub_ref, q_ref, k1c_ref, k2c_ref, vc_ref = refs[:9]
    if n_chunks:
        k1_ref, k2_ref, v_ref, o_ref, s_scr = refs[9:]
    else:
        (o_ref, s_scr), k1_ref, k2_ref, v_ref = refs[9:], None, None, None
    lam = (jnp.exp(jnp.sum(lq1_ref[...] * lk1_ref[...], axis=-1, keepdims=True))
           - jnp.exp(jnp.sum(lq2_ref[...] * lk2_ref[...], axis=-1, keepdims=True)) + lam_init)
    kc_refs, k_refs = (k1c_ref, k2c_ref), (k1_ref, k2_ref)
    ext = lambda v: jnp.concatenate([v, jnp.ones_like(v)], axis=1)
    (_, a1), (_, a2) = _flash(
        2, n_chunks, tk,
        score_ctx=lambda t: _qk(q_ref[0], kc_refs[t][0]),
        score=lambda t, rows: _qk(q_ref[0], k_refs[t][0, rows, :]),
        value_ctx=lambda t: ext(vc_ref[0]),
        value=lambda t, rows: ext(v_ref[0, rows, :]),
        s_scr=s_scr, exp_dtype=F32)
    o = a1[:, :LANES] / a1[:, LANES:] - lam * (a2[:, :LANES] / a2[:, LANES:])
    o_ref[0] = (_rms(o, gsub_ref[...]) * (1.0 - lam_init)).astype(o_ref.dtype)


def _diff(lams, g_sub, q, k1c, k2c, vc, k1=None, k2=None, v=None, *, tq, tk, lam_init):
    B, Sq, _ = q.shape
    C = k1c.shape[1]
    small = lambda n: pl.BlockSpec((1, n), lambda b, h, i: (0, 0))
    full = lambda n: pl.BlockSpec((1, n, LANES), lambda b, h, i: (b, 0, h))
    tile = pl.BlockSpec((1, tq, LANES), lambda b, h, i: (b, i, h))
    ins = list(lams) + [g_sub, q, k1c, k2c, vc]
    specs = [small(DH_B)] * 4 + [small(V_B), tile, full(C), full(C), full(C)]
    n_chunks = 0
    if k1 is not None:
        S = k1.shape[1]
        ins += [k1, k2, v]
        specs += [full(S)] * 3
        n_chunks = S // tk
    return pl.pallas_call(
        functools.partial(_diff_kernel, n_chunks=n_chunks, tk=tk, lam_init=lam_init),
        grid=(B, H_B, Sq // tq),
        in_specs=specs,
        out_specs=tile,
        out_shape=jax.ShapeDtypeStruct((B, Sq, H_B * V_B), BF16),
        scratch_shapes=[_score_scratch(n_chunks, tq, tk)],
        compiler_params=_params(3),
        name="diff",
    )(*ins)


def _merge_kernel(x_ref, mod_ref, oa_ref, ob_ref, sga_ref, sgb_ref, wa_ref, wb_ref, wo_ref, o_ref):
    dot = functools.partial(jnp.dot, preferred_element_type=F32)
    ya = dot(oa_ref[0], wa_ref[...])
    yb = dot(ob_ref[0], wb_ref[...])
    merged = sga_ref[0].astype(F32) * ya + sgb_ref[0].astype(F32) * yb
    o_ref[0] = x_ref[0] + mod_ref[0][2:3] * dot(merged.astype(BF16), wo_ref[...])


def _merge(x, mod_l, mod_row, oa, ob, sga, sgb, w, tm):
    B, S, D = x.shape
    const = lambda shape: pl.BlockSpec(shape, lambda b, i: (0,) * len(shape))
    tok = lambda n: pl.BlockSpec((1, tm, n), lambda b, i: (b, i, 0))
    na, nb = H_A * V_A, H_B * V_B
    return pl.pallas_call(
        _merge_kernel,
        grid=(B, S // tm),
        in_specs=[tok(D), pl.BlockSpec((1, 6, D), lambda b, i: (mod_row(b), 0, 0)),
                  tok(na), tok(nb), tok(D), tok(D),
                  const((na, D)), const((nb, D)), const((D, D))],
        out_specs=tok(D),
        out_shape=jax.ShapeDtypeStruct((B, S, D), F32),
        compiler_params=_params(2),
        name="merge",
    )(x, mod_l, oa, ob, sga, sgb, w["w_br_a"], w["w_br_b"], w["w_out"])


def _ffn_kernel(*refs, tm, final):
    xp_ref, x_ref, xn_ref, mod_ref, g_ref, wa_ref, wb_ref, cwa_ref, cwb_ref, wd_ref = refs[:10]
    gfin_ref = refs[10] if final else None
    o_ref = refs[-1]
    dot = functools.partial(jnp.dot, preferred_element_type=F32)
    i = pl.program_id(1)
    mod = mod_ref[0]
    norm = lambda v: _rms(v, g_ref[...]) * (1.0 + mod[4:5]) + mod[3:4]
    x = x_ref[0]
    keep_prev = (i > 0).astype(F32)
    keep_next = (i < pl.num_programs(1) - 1).astype(F32)
    h = jnp.concatenate([norm(xp_ref[0]) * keep_prev, norm(x), norm(xn_ref[0]) * keep_next],
                        axis=0).astype(BF16)
    half = tm // 2
    rows = half + 2 * HALO

    def conv(u, cw):
        mid = slice(HALO, HALO + half)
        return (pltpu.roll(u, 1, 0)[mid] * cw[0:1] + u[mid] * cw[1:2]
                + pltpu.roll(u, rows - 1, 0)[mid] * cw[2:3] + cw[3:4])

    ys = []
    for off in (0, half):
        hw = h[off:off + rows]
        a = conv(dot(hw, wa_ref[...]), cwa_ref[...])
        b = conv(dot(hw, wb_ref[...]), cwb_ref[...])
        ys.append(dot((a * jax.nn.sigmoid(a) * b).astype(BF16), wd_ref[...]))
    out = x + mod[5:6] * jnp.concatenate(ys, axis=0)
    if final:
        out = _rms(out, gfin_ref[...])
    o_ref[0] = out


def _ffn(x, mod_l, mod_row, g_ffn, w, tm, g_final=None):
    B, S, D = x.shape
    F = w["w_down"].shape[0]
    const = lambda shape: pl.BlockSpec(shape, lambda b, i: (0,) * len(shape))
    r = tm // HALO
    last = S // HALO - 1
    ins = [x, x, x, mod_l, g_ffn, w["w_up_a"], w["w_up_b"], w["conv_a"], w["conv_b"], w["w_down"]]
    specs = [pl.BlockSpec((1, HALO, D), lambda b, i: (b, jnp.maximum(i * r - 1, 0), 0)),
             pl.BlockSpec((1, tm, D), lambda b, i: (b, i, 0)),
             pl.BlockSpec((1, HALO, D), lambda b, i: (b, jnp.minimum((i + 1) * r, last), 0)),
             pl.BlockSpec((1, 6, D), lambda b, i: (mod_row(b), 0, 0)),
             const((1, D)), const((D, F)), const((D, F)),
             const((CONV_W + 1, F)), const((CONV_W + 1, F)), const((F, D))]
    if g_final is not None:
        ins.append(g_final)
        specs.append(const((1, D)))
    return pl.pallas_call(
        functools.partial(_ffn_kernel, tm=tm, final=g_final is not None),
        grid=(B, S // tm),
        in_specs=specs,
        out_specs=pl.BlockSpec((1, tm, D), lambda b, i: (b, i, 0)),
        out_shape=jax.ShapeDtypeStruct((B, S, D), F32),
        compiler_params=_params(2),
        name="ffn",
    )(*ins)


def _slot_pad(w, heads, width, left=0):
    k = w.shape[0]
    w = w.reshape(k, heads, width)
    w = jnp.pad(w, ((0, 0), (0, 0), (left, LANES - width - left)))
    return w.reshape(k, heads * LANES)


def _layer_weights(l, w_in, g_q, w_uq, g_kv, w_ukv, w_br_a, w_br_b, w_out, w_up, conv_w, conv_b, w_down):
    D, F = D_MODEL, D_FF
    cuts = [0, Q_LORA, Q_LORA + KV_LORA, Q_LORA + KV_LORA + ROPE_A]
    wi = w_in[l]
    kr = jnp.pad(wi[:, cuts[2]:cuts[3]], ((0, 0), (NOPE_A, LANES - NOPE_A - ROPE_A)))
    w_in_p = jnp.concatenate([wi[:, :cuts[2]], kr, wi[:, cuts[3]:]], axis=1)
    ukv = w_ukv[l].reshape(KV_LORA, H_A, NOPE_A + V_A)
    kn, va = ukv[..., :NOPE_A], ukv[..., NOPE_A:]
    odd = (jnp.arange(H_A) % 2 == 1)[None, :, None]
    v_slots = jnp.where(odd, jnp.pad(va, ((0, 0), (0, 0), (LANES - V_A, 0))),
                        jnp.pad(va, ((0, 0), (0, 0), (0, LANES - V_A))))
    cw = jnp.concatenate([conv_w[l], conv_b[l][None]], axis=0)
    return {
        "w_in": w_in_p.astype(BF16),
        "g_q": g_q[l][None], "g_kv": g_kv[l][None],
        "w_uq": _slot_pad(w_uq[l], H_A, NOPE_A + ROPE_A).astype(BF16),
        "w_k": _slot_pad(kn.reshape(KV_LORA, H_A * NOPE_A), H_A, NOPE_A).astype(BF16),
        "w_v": v_slots.reshape(KV_LORA, H_A * LANES).astype(BF16),
        "w_br_a": w_br_a[l].astype(BF16), "w_br_b": w_br_b[l].astype(BF16), "w_out": w_out[l].astype(BF16),
        "w_up_a": w_up[l][:, :F].astype(BF16), "w_up_b": w_up[l][:, F:].astype(BF16),
        "conv_a": cw[:, :F], "conv_b": cw[:, F:],
        "w_down": w_down[l].astype(BF16),
    }


def _rope_tables(S, rotary):
    def slot_tables(dr, left, reps):
        q = dr // 4
        if rotary:
            t = jnp.arange(S)
            rows = (t // GRID_W).astype(F32)
            cols = (t % GRID_W).astype(F32)
            freqs = ROPE_BASE ** (-jnp.arange(q, dtype=F32) / q)
            ar, ac = rows[:, None] * freqs, cols[:, None] * freqs
            cos = jnp.concatenate([jnp.cos(ar), jnp.cos(ar), jnp.cos(ac), jnp.cos(ac)], axis=-1)
            sin = jnp.concatenate([-jnp.sin(ar), jnp.sin(ar), -jnp.sin(ac), jnp.sin(ac)], axis=-1)
        else:
            cos, sin = jnp.ones((S, dr), F32), jnp.zeros((S, dr), F32)
        cos, sin = jnp.tile(cos, (1, reps)), jnp.tile(sin, (1, reps))
        pad = ((0, 0), (left, LANES - left - dr * reps))
        return jnp.pad(cos, pad, constant_values=1.0), jnp.pad(sin, pad)

    ca, sa = slot_tables(ROPE_A, NOPE_A, 1)
    cb, sb = slot_tables(DH_B, 0, 2)
    scale_a = (NOPE_A + ROPE_A) ** -0.5 * math.log2(math.e)
    scale_b = DH_B ** -0.5 * math.log2(math.e)
    return jnp.concatenate([ca * scale_a, sa * scale_a, ca, sa, cb * scale_b, sb * scale_b, cb, sb], axis=1)


class _Tiles(NamedTuple):
    proj: int
    merge: int
    ffn: int
    mla_q: int
    mla_k: int
    diff_q: int
    diff_k: int


def _tiles(S):
    return _Tiles(*(min(v, S) for v in (256, 512, 512, 1024, 1024, 1024, 512)))


def kernel(x, c, ctx, c_ctx, w_ada, b_ada, g_mix, g_ffn, w_in, g_q, w_uq, g_kv, w_ukv, lam_q1, lam_k1, lam_q2,
           lam_k2, g_sub, w_br_a, w_br_b, w_out, w_up, conv_w, conv_b, w_down, g_final):
    B, S, D = x.shape
    C = ctx.shape[1]
    depth = w_in.shape[0]
    t = _tiles(S)

    ctx_row = B
    n_rows = -(-(B + 1) // 8) * 8
    cc = jnp.zeros((n_rows, D), F32).at[:B].set(c).at[ctx_row].set(c_ctx)
    mod = _ada(cc, w_ada, b_ada).reshape(depth, n_rows, 6, D)
    lat_row = lambda b: b
    ctx_mod_row = lambda b: ctx_row

    tab_lat = _rope_tables(S, True)
    tab_ctx = _rope_tables(C, False)

    x_lat, x_ctx = x, ctx
    for l in range(depth):
        last = l == depth - 1
        lam_init = 0.8 - 0.6 * math.exp(-0.3 * l)
        w = _layer_weights(l, w_in, g_q, w_uq, g_kv, w_ukv, w_br_a, w_br_b, w_out, w_up, conv_w, conv_b,
                           w_down)
        lams = [v[l][None] for v in (lam_q1, lam_k1, lam_q2, lam_k2)]
        gs = g_sub[l][None]
        gm, gf = g_mix[l][None], g_ffn[l][None]

        pc = _proj(x_ctx, mod[l], ctx_mod_row, gm, w, tab_ctx, min(t.proj, C))
        pq = _proj(x_lat, mod[l], lat_row, gm, w, tab_lat, t.proj)
        qa, ka, va, qb, kb1, kb2, vb, sga, sgb = pq
        cqa, cka, cva, cqb, ckb1, ckb2, cvb, csga, csgb = pc

        o_a = _mla(qa, cka, cva, ka, va, tq=t.mla_q, tk=t.mla_k)
        o_b = _diff(lams, gs, qb, ckb1, ckb2, cvb, kb1, kb2, vb, tq=t.diff_q, tk=t.diff_k, lam_init=lam_init)
        if not last:
            oc_a = _mla(cqa, cka, cva, tq=C, tk=C)
            oc_b = _diff(lams, gs, cqb, ckb1, ckb2, cvb, tq=C, tk=C, lam_init=lam_init)
            x_ctx = _merge(x_ctx, mod[l], ctx_mod_row, oc_a, oc_b, csga, csgb, w, C)
            x_ctx = _ffn(x_ctx, mod[l], ctx_mod_row, gf, w, C)
        x_lat = _merge(x_lat, mod[l], lat_row, o_a, o_b, sga, sgb, w, t.merge)
        x_lat = _ffn(x_lat, mod[l], lat_row, gf, w, t.ffn, g_final[None] if last else None)
    return x_lat
```
